```python
import jax, jax.numpy as jnp
from jax import lax
import numpy as np

D_MODEL = 2048
BATCH = 4
SEQ = 4096
DEPTH = 4

CTX_LEN = 256
GRID_W = 64
N_MIXERS = 2
N_HEADS = 16
HEAD_DIM = D_MODEL // N_HEADS
WIN_ROWS = 8
WIN_COLS = 16
Q_COLS = 16
K_COLS = Q_COLS + WIN_COLS
POOL_WINDOWS = (2, 4, 8, 16)
N_POOL_GROUPS = len(POOL_WINDOWS)
POOL_GROUP = D_MODEL // N_POOL_GROUPS
D_FF = -(-8 * D_MODEL // (3 * 256)) * 256
N_ADA = 6
LN_EPS = 1e-5
DN_ALPHA = (2 * DEPTH) ** 0.25
DN_BETA = (8 * DEPTH) ** -0.25
N_NA_LAYERS = len([i for i in range(DEPTH) if i % N_MIXERS == 0])
N_POOL_LAYERS = DEPTH - N_NA_LAYERS

kernel_name = "hybrid_natten_pool_deepnorm_dit"


def layer_norm(x, g, b):
    xf = x.astype(jnp.float32)
    mu = xf.mean(-1, keepdims=True)
    var = jnp.square(xf - mu).mean(-1, keepdims=True)
    y = (xf - mu) * lax.rsqrt(var + LN_EPS) * g.astype(jnp.float32) + b.astype(jnp.float32)
    return y.astype(x.dtype)


def ada_params(cond, w, b):
    return jnp.split(jax.nn.silu(cond) @ w + b, N_ADA, axis=-1)


def modulate(x, shift, scale):
    return x * (1 + scale) + shift


def split_heads(t):
    return t.reshape(*t.shape[:-1], N_HEADS, HEAD_DIM)


def neighbourhood_attention(q, k, v, k_ctx, v_ctx, rpb):
    B, L = q.shape[0], q.shape[1]
    rows = L // GRID_W
    kr = min(WIN_ROWS, rows)
    n_cb = GRID_W // Q_COLS
    scale = HEAD_DIM ** -0.5
    q_col = np.arange(GRID_W).reshape(n_cb, Q_COLS)
    win_start = np.clip(q_col - WIN_COLS // 2, 0, GRID_W - WIN_COLS)
    band_start = np.clip(np.arange(n_cb) * Q_COLS - WIN_COLS // 2, 0, GRID_W - K_COLS)
    key_col_np = band_start[:, None] + np.arange(K_COLS)
    kc = key_col_np[:, None, :]
    col_ok = (kc >= win_start[:, :, None]) & (kc < win_start[:, :, None] + WIN_COLS)
    col_ok = jnp.asarray(col_ok)[:, :, None, :]
    dc_idx = jnp.asarray(np.clip(kc - q_col[:, :, None] + WIN_COLS - 1, 0, 2 * WIN_COLS - 2))[:, :, None, :]
    key_col = jnp.asarray(key_col_np)
    qg = q.reshape(B, rows, GRID_W, N_HEADS, HEAD_DIM)
    kg = k.reshape(B, rows, GRID_W, N_HEADS, HEAD_DIM)
    vg = v.reshape(B, rows, GRID_W, N_HEADS, HEAD_DIM)
    n_win = kr * K_COLS

    def row_block(r):
        r0 = jnp.clip(r - kr // 2, 0, rows - kr)
        q_blk = lax.dynamic_index_in_dim(qg, r, axis=1, keepdims=False).reshape(B, n_cb, Q_COLS, N_HEADS, HEAD_DIM)
        k_blk = lax.dynamic_slice_in_dim(kg, r0, kr, axis=1)[:, :, key_col]
        v_blk = lax.dynamic_slice_in_dim(vg, r0, kr, axis=1)[:, :, key_col]
        dr_idx = (r0 + jnp.arange(kr) - r + WIN_ROWS - 1)[None, None, :, None]
        bias = rpb[:, dr_idx, dc_idx]
        s_win = jnp.einsum('bjqhd,brjkhd->bhjqrk', q_blk, k_blk).astype(jnp.float32) * scale
        s_win = jnp.where(col_ok, s_win + bias.astype(jnp.float32), -jnp.inf)
        s_win = s_win.reshape(B, N_HEADS, n_cb, Q_COLS, n_win)
        s_ctx = jnp.einsum('bjqhd,bchd->bhjqc', q_blk, k_ctx).astype(jnp.float32) * scale
        p = jax.nn.softmax(jnp.concatenate([s_win, s_ctx], axis=-1), axis=-1).astype(v.dtype)
        p_win = p[..., :n_win].reshape(B, N_HEADS, n_cb, Q_COLS, kr, K_COLS)
        p_ctx = p[..., n_win:]
        o = (jnp.einsum('bhjqrk,brjkhd->bjqhd', p_win, v_blk)
             + jnp.einsum('bhjqc,bchd->bjqhd', p_ctx, v_ctx))
        return o.reshape(B, GRID_W, D_MODEL)

    o = lax.map(row_block, jnp.arange(rows))
    return jnp.moveaxis(o, 0, 1).reshape(B, L, D_MODEL)


def context_attention(q_ctx, k_ctx, v_ctx):
    B, C = q_ctx.shape[0], q_ctx.shape[1]
    s = jnp.einsum('bqhd,bkhd->bhqk', q_ctx, k_ctx).astype(jnp.float32) * HEAD_DIM ** -0.5
    p = jax.nn.softmax(s, axis=-1).astype(v_ctx.dtype)
    return jnp.einsum('bhqk,bkhd->bqhd', p, v_ctx).reshape(B, C, D_MODEL)


def multiscale_pool(h, w_pool, scale):
    B, L, _ = h.shape
    hf = h.astype(jnp.float32)
    csum = jnp.concatenate([jnp.zeros_like(hf[:, :1]), lax.cumsum(hf, axis=1)], axis=1)
    t = np.arange(L)
    groups = []
    for g, w in enumerate(POOL_WINDOWS):
        lo = np.clip(t - w // 2, 0, L)
        hi = np.clip(t - w // 2 + w, 0, L)
        sl = slice(g * POOL_GROUP, (g + 1) * POOL_GROUP)
        cg = csum[..., sl]
        mean = (cg[:, hi] - cg[:, lo]) / jnp.asarray(hi - lo, dtype=jnp.float32)[:, None]
        groups.append(mean - hf[..., sl])
    pooled = jnp.stack(groups, axis=2).astype(h.dtype)
    y = jnp.einsum('blgc,gcd->blgd', pooled, w_pool).reshape(B, L, D_MODEL)
    return y * scale


def swiglu(h, w_in, w_out):
    gate, up = jnp.split(h @ w_in, 2, axis=-1)
    return (jax.nn.silu(gate) * up) @ w_out


def setup_inputs(seed: int = 0) -> dict:
    key = jax.random.key(seed)
    ks = jax.random.split(key, 20)
    f32 = jnp.float32

    def nrm(k, shape, s):
        return jax.random.normal(k, shape, f32) * s

    v_scale = jnp.concatenate([jnp.ones((2 * D_MODEL,), f32), jnp.full((D_MODEL,), DN_BETA, f32)])
    return {
        "x": nrm(ks[0], (BATCH, SEQ, D_MODEL), 1.0),
        "c": nrm(ks[1], (BATCH, D_MODEL), 1.0),
        "ctx": nrm(ks[2], (BATCH, CTX_LEN, D_MODEL), 1.0),
        "c_ctx": nrm(ks[3], (D_MODEL,), 1.0),
        "ada_w": nrm(ks[4], (DEPTH, D_MODEL, N_ADA * D_MODEL), 0.5 * D_MODEL ** -0.5),
        "ada_b": nrm(ks[5], (DEPTH, N_ADA * D_MODEL), 0.02),
        "ln_mix_g": 1.0 + nrm(ks[6], (DEPTH, D_MODEL), 0.02),
        "ln_mix_b": nrm(ks[7], (DEPTH, D_MODEL), 0.02),
        "ln_ffn_g": 1.0 + nrm(ks[8], (DEPTH, D_MODEL), 0.02),
        "ln_ffn_b": nrm(ks[9], (DEPTH, D_MODEL), 0.02),
        "na_w_qkv": nrm(ks[10], (N_NA_LAYERS, D_MODEL, 3 * D_MODEL), D_MODEL ** -0.5) * v_scale,
        "na_w_o": nrm(ks[11], (N_NA_LAYERS, D_MODEL, D_MODEL), DN_BETA * D_MODEL ** -0.5),
        "na_rpb": nrm(ks[12], (N_NA_LAYERS, N_HEADS, 2 * WIN_ROWS - 1, 2 * WIN_COLS - 1), 0.1),
        "pool_w": nrm(ks[13], (N_POOL_LAYERS, N_POOL_GROUPS, POOL_GROUP, POOL_GROUP), DN_BETA * POOL_GROUP ** -0.5),
        "pool_scale": 1.0 + nrm(ks[14], (N_POOL_LAYERS, D_MODEL), 0.05),
        "ffn_w_in": nrm(ks[15], (DEPTH, D_MODEL, 2 * D_FF), D_MODEL ** -0.5),
        "ffn_w_out": nrm(ks[16], (DEPTH, D_FF, D_MODEL), DN_BETA * D_FF ** -0.5),
    }


def reference(x, c, ctx, c_ctx, ada_w, ada_b, ln_mix_g, ln_mix_b, ln_ffn_g, ln_ffn_b,
              na_w_qkv, na_w_o, na_rpb, pool_w, pool_scale, ffn_w_in, ffn_w_out):
    last_na = max(i for i in range(DEPTH) if i % N_MIXERS == 0)
    xc = ctx
    for i in range(DEPTH):
        use_na = i % N_MIXERS == 0
        j = i // N_MIXERS
        update_ctx = i < last_na
        sh1, sc1, g1, sh2, sc2, g2 = [m[:, None, :] for m in ada_params(c, ada_w[i], ada_b[i])]
        h = modulate(x, sh1, sc1)
        if update_ctx or use_na:
            csh1, csc1, cg1, csh2, csc2, cg2 = ada_params(c_ctx, ada_w[i], ada_b[i])
            hc = modulate(xc, csh1, csc1)
        if use_na:
            w_qkv = na_w_qkv[j]
            q, k, v = [split_heads(t) for t in jnp.split(h @ w_qkv, 3, axis=-1)]
            k_c, v_c = [split_heads(t) for t in jnp.split(hc @ w_qkv[:, D_MODEL:], 2, axis=-1)]
            y = neighbourhood_attention(q, k, v, k_c, v_c, na_rpb[j]) @ na_w_o[j]
            if update_ctx:
                q_c = split_heads(hc @ w_qkv[:, :D_MODEL])
                yc = context_attention(q_c, k_c, v_c) @ na_w_o[j]
        else:
            y = multiscale_pool(h, pool_w[j], pool_scale[j])
            if update_ctx:
                yc = multiscale_pool(hc, pool_w[j], pool_scale[j])
        x = layer_norm(DN_ALPHA * x + g1 * y, ln_mix_g[i], ln_mix_b[i])
        x = layer_norm(DN_ALPHA * x + g2 * swiglu(modulate(x, sh2, sc2), ffn_w_in[i], ffn_w_out[i]),
                       ln_ffn_g[i], ln_ffn_b[i])
        if update_ctx:
            xc = layer_norm(DN_ALPHA * xc + cg1 * yc, ln_mix_g[i], ln_mix_b[i])
            xc = layer_norm(DN_ALPHA * xc + cg2 * swiglu(modulate(xc, csh2, csc2), ffn_w_in[i], ffn_w_out[i]),
                            ln_ffn_g[i], ln_ffn_b[i])
    return x
```

```python
import functools
import math

import numpy as np
import jax
import jax.numpy as jnp
from jax import lax
from jax.experimental import pallas as pl
from jax.experimental.pallas import tpu as pltpu

F32 = jnp.float32
BF16 = jnp.bfloat16

GRID_W = 64
N_HEADS = 16
HEAD_DIM = 128
WIN_ROWS = 8
WIN_COLS = 16
POOL_WINDOWS = (2, 4, 8, 16)
N_ADA = 6
LN_EPS = 1e-5
LOG2E = math.log2(math.e)

Q_ROWS = 4
K_ROWS = Q_ROWS + WIN_ROWS
Q_TOK = Q_ROWS * GRID_W
K_TOK = K_ROWS * GRID_W
N_BIAS_TILES = 18
POOL_HALO = 8
COND_ROWS = 8

VMEM_LIMIT = 56 * 1024 * 1024


def _cparams(*sem):
    return pltpu.CompilerParams(dimension_semantics=sem, vmem_limit_bytes=VMEM_LIMIT)


def _layer_norm(v, g, b):
    mu = jnp.mean(v, axis=-1, keepdims=True)
    d = v - mu
    var = jnp.mean(d * d, axis=-1, keepdims=True)
    return d * lax.rsqrt(var + LN_EPS) * g + b


def _silu(v):
    return v * jax.nn.sigmoid(v)


def _ada_kernel(cond_ref, w_ref, b_ref, o_ref):
    s = _silu(cond_ref[...]).astype(BF16)
    o_ref[...] = jnp.dot(s, w_ref[...].astype(BF16), preferred_element_type=F32) + b_ref[...]


def _ada_params(cond, ada_w, ada_b, tn=1024):
    depth, d, n = ada_w.shape
    return pl.pallas_call(
        _ada_kernel,
        grid=(depth, n // tn),
        in_specs=[
            pl.BlockSpec((COND_ROWS, d), lambda l, j: (0, 0)),
            pl.BlockSpec((None, d, tn), lambda l, j: (l, 0, j)),
            pl.BlockSpec((None, 1, tn), lambda l, j: (l, 0, j)),
        ],
        out_specs=pl.BlockSpec((None, COND_ROWS, tn), lambda l, j: (l, 0, j)),
        out_shape=jax.ShapeDtypeStruct((depth, COND_ROWS, n), F32),
        compiler_params=_cparams("parallel", "parallel"),
        name="ada_params",
    )(cond, ada_w, ada_b.reshape(depth, 1, n))


def _ada_spec(layer, which, row_fn, d):
    return pl.BlockSpec((None, None, None, 1, d), lambda i, *_: (layer, row_fn(i), which, 0, 0))


def _vec_spec(d):
    return pl.BlockSpec((1, d), lambda i, *_: (0, 0))


def _qkv_kernel(x_ref, sh_ref, sc_ref, w_ref, qk_ref, vt_ref, h_ref, *, n_qk_tiles, heads_per_tile):
    j = pl.program_id(1)

    @pl.when(j == 0)
    def _():
        h_ref[...] = (x_ref[...] * (1.0 + sc_ref[...]) + sh_ref[...]).astype(BF16)

    res = jnp.dot(h_ref[...], w_ref[...], preferred_element_type=F32)

    @pl.when(j < n_qk_tiles)
    def _():
        for hh in range(heads_per_tile):
            qk_ref[hh] = res[:, hh * HEAD_DIM:(hh + 1) * HEAD_DIM].astype(BF16)

    @pl.when(j >= n_qk_tiles)
    def _():
        for hh in range(heads_per_tile):
            vt_ref[hh] = res[:, hh * HEAD_DIM:(hh + 1) * HEAD_DIM].T.astype(BF16)


def _qkv_proj(x, ada, layer, row_fn, w_qkv, tm, tn=512):
    m, d = x.shape
    n = w_qkv.shape[1]
    hpt = tn // HEAD_DIM
    n_qk_tiles = (2 * d) // tn
    kern = functools.partial(_qkv_kernel, n_qk_tiles=n_qk_tiles, heads_per_tile=hpt)
    return pl.pallas_call(
        kern,
        grid=(m // tm, n // tn),
        in_specs=[
            pl.BlockSpec((tm, d), lambda i, j: (i, 0)),
            _ada_spec(layer, 0, row_fn, d),
            _ada_spec(layer, 1, row_fn, d),
            pl.BlockSpec((d, tn), lambda i, j: (0, j)),
        ],
        out_specs=[
            pl.BlockSpec((hpt, tm, HEAD_DIM), lambda i, j: (jnp.minimum(j, n_qk_tiles - 1), i, 0)),
            pl.BlockSpec((hpt, HEAD_DIM, tm), lambda i, j: (jnp.maximum(j - n_qk_tiles, 0), 0, i)),
        ],
        out_shape=[
            jax.ShapeDtypeStruct((2 * N_HEADS, m, HEAD_DIM), BF16),
            jax.ShapeDtypeStruct((N_HEADS, HEAD_DIM, m), BF16),
        ],
        scratch_shapes=[pltpu.VMEM((tm, d), BF16)],
        compiler_params=_cparams("parallel", "arbitrary"),
        name="qkv_proj",
    )(x, ada, ada, w_qkv)


def _bias_tiles(rpb):
    e = np.arange(N_BIAS_TILES)[:, None, None]
    ck = np.arange(GRID_W)[None, :, None]
    lane = np.arange(2 * GRID_W)[None, None, :]
    half, cq = lane // GRID_W, lane % GRID_W
    d = np.where(e == 16, -4, np.where(e == 17, 4, e - (WIN_ROWS - 1)))
    dr = d - half
    row_ok = (np.abs(dr) <= WIN_ROWS - 1) & ((e != 16) | (half == 0)) & ((e != 17) | (half == 1))
    ws = np.clip(cq - WIN_COLS // 2, 0, GRID_W - WIN_COLS)
    ok = row_ok & (ck >= ws) & (ck < ws + WIN_COLS)
    dr_idx = np.broadcast_to(np.clip(dr + WIN_ROWS - 1, 0, 2 * WIN_ROWS - 2), ok.shape)
    dc_idx = np.broadcast_to(np.clip(ck - cq + WIN_COLS - 1, 0, 2 * WIN_COLS - 2), ok.shape)
    vals = rpb[:, dr_idx, dc_idx] * LOG2E
    return jnp.where(jnp.asarray(ok)[None], vals, -jnp.inf).astype(F32)


def _tile_entry(d):
    return d + WIN_ROWS - 1


_TILES_TOP = [[(j, _tile_entry(j - 2 * u)) for j in range(0, WIN_ROWS)] for u in range(Q_ROWS // 2)]
_TILES_BOT = [[(j, _tile_entry(j - WIN_ROWS - 2 * u)) for j in range(K_ROWS - WIN_ROWS, K_ROWS)]
              for u in range(Q_ROWS // 2)]
_TILES_MID = [[(j, 16 if j - 2 * u - 4 == -4 else 17 if j - 2 * u - 4 == 4 else _tile_entry(j - 2 * u - 4))
               for j in range(2 * u, 2 * u + WIN_ROWS + 1)] for u in range(Q_ROWS // 2)]


def _na_kernel(q_ref, k_ref, vt_ref, kc_ref, vct_ref, tab_ref, o_ref, st_ref, pt_ref, *, n_blocks, n_ctx):
    lanes = 2 * GRID_W

    def block(q0, k0, tiles):
        q = q_ref[pl.ds(q0, Q_TOK), :]
        nt = (((1,), (1,)), ((), ()))
        st_ref[0:K_TOK, :] = lax.dot_general(k_ref[pl.ds(k0, K_TOK), :], q, nt, preferred_element_type=F32)
        st_ref[K_TOK:K_TOK + n_ctx, :] = lax.dot_general(kc_ref[...], q, nt, preferred_element_type=F32)
        inv_l = []
        for u in range(Q_ROWS // 2):
            cols = slice(u * lanes, (u + 1) * lanes)
            used = dict(tiles[u])
            sc = [st_ref[j * GRID_W:(j + 1) * GRID_W, cols] + tab_ref[e] for j, e in tiles[u]]
            sc += [st_ref[K_TOK + t * GRID_W:K_TOK + (t + 1) * GRID_W, cols] for t in range(n_ctx // GRID_W)]
            m = functools.reduce(jnp.maximum, sc)
            m = jnp.max(m, axis=0, keepdims=True)
            ps = [jnp.exp2(s - m) for s in sc]
            l = jnp.sum(functools.reduce(jnp.add, ps), axis=0, keepdims=True)
            inv_l.append(1.0 / l)
            rows = [j for j, _ in tiles[u]] + [K_ROWS + t for t in range(n_ctx // GRID_W)]
            for j, p in zip(rows, ps):
                pt_ref[j * GRID_W:(j + 1) * GRID_W, cols] = p.astype(BF16)
            for j in range(K_ROWS):
                if j not in used:
                    pt_ref[j * GRID_W:(j + 1) * GRID_W, cols] = jnp.zeros((GRID_W, lanes), BF16)
        ot = jnp.dot(vt_ref[:, pl.ds(k0, K_TOK)], pt_ref[0:K_TOK, :], preferred_element_type=F32)
        ot += jnp.dot(vct_ref[...], pt_ref[K_TOK:K_TOK + n_ctx, :], preferred_element_type=F32)
        ot = ot * jnp.concatenate(inv_l, axis=1)
        o_ref[pl.ds(q0, Q_TOK), :] = ot.T.astype(BF16)

    block(0, 0, _TILES_TOP)

    def body(i, carry):
        q0 = pl.multiple_of(i * Q_TOK, Q_TOK)
        k0 = pl.multiple_of((i - 1) * Q_TOK, Q_TOK)
        block(q0, k0, _TILES_MID)
        return carry

    lax.fori_loop(1, n_blocks - 1, body, 0)
    block((n_blocks - 1) * Q_TOK, n_blocks * Q_TOK - K_TOK, _TILES_BOT)


def _neighbourhood_attention(qk, vt, qk_c, vt_c, tab, batch):
    m = qk.shape[1]
    seq = m // batch
    n_ctx = qk_c.shape[1] // batch
    n_blocks = seq // Q_TOK
    kern = functools.partial(_na_kernel, n_blocks=n_blocks, n_ctx=n_ctx)
    return pl.pallas_call(
        kern,
        grid=(batch, N_HEADS),
        in_specs=[
            pl.BlockSpec((None, seq, HEAD_DIM), lambda b, h: (h, b, 0)),
            pl.BlockSpec((None, seq, HEAD_DIM), lambda b, h: (N_HEADS + h, b, 0)),
            pl.BlockSpec((None, HEAD_DIM, seq), lambda b, h: (h, 0, b)),
            pl.BlockSpec((None, n_ctx, HEAD_DIM), lambda b, h: (N_HEADS + h, b, 0)),
            pl.BlockSpec((None, HEAD_DIM, n_ctx), lambda b, h: (h, 0, b)),
            pl.BlockSpec((None, N_BIAS_TILES, GRID_W, 2 * GRID_W), lambda b, h: (h, 0, 0, 0)),
        ],
        out_specs=pl.BlockSpec((None, seq, HEAD_DIM), lambda b, h: (h, b, 0)),
        out_shape=jax.ShapeDtypeStruct((N_HEADS, m, HEAD_DIM), BF16),
        scratch_shapes=[
            pltpu.VMEM((K_TOK + n_ctx, Q_TOK), F32),
            pltpu.VMEM((K_TOK + n_ctx, Q_TOK), BF16),
        ],
        compiler_params=_cparams("parallel", "parallel"),
        name="neighbourhood_attention",
    )(qk, qk, vt, qk_c, vt_c, tab)


def _ctx_attn_kernel(q_ref, k_ref, vt_ref, o_ref):
    nt = (((1,), (1,)), ((), ()))
    st = lax.dot_general(k_ref[...], q_ref[...], nt, preferred_element_type=F32)
    p = jnp.exp2(st - jnp.max(st, axis=0, keepdims=True))
    inv_l = 1.0 / jnp.sum(p, axis=0, keepdims=True)
    ot = jnp.dot(vt_ref[...], p.astype(BF16), preferred_element_type=F32) * inv_l
    o_ref[...] = ot.T.astype(BF16)


def _context_attention(qk_c, vt_c, batch):
    m = qk_c.shape[1]
    n_ctx = m // batch
    return pl.pallas_call(
        _ctx_attn_kernel,
        grid=(batch, N_HEADS),
        in_specs=[
            pl.BlockSpec((None, n_ctx, HEAD_DIM), lambda b, h: (h, b, 0)),
            pl.BlockSpec((None, n_ctx, HEAD_DIM), lambda b, h: (N_HEADS + h, b, 0)),
            pl.BlockSpec((None, HEAD_DIM, n_ctx), lambda b, h: (h, 0, b)),
        ],
        out_specs=pl.BlockSpec((None, n_ctx, HEAD_DIM), lambda b, h: (h, b, 0)),
        out_shape=jax.ShapeDtypeStruct((N_HEADS, m, HEAD_DIM), BF16),
        compiler_params=_cparams("parallel", "parallel"),
        name="context_attention",
    )(qk_c, qk_c, vt_c)


def _proj_ln_kernel(a_ref, w_ref, x_ref, g_ref, lng_ref, lnb_ref, o_ref, acat_ref, *, alpha):
    for h in range(N_HEADS):
        acat_ref[:, h * HEAD_DIM:(h + 1) * HEAD_DIM] = a_ref[h]
    y = jnp.dot(acat_ref[...], w_ref[...], preferred_element_type=F32)
    o_ref[...] = _layer_norm(alpha * x_ref[...] + g_ref[...] * y, lng_ref[...], lnb_ref[...])


def _proj_ln(a, w_o, x, ada, layer, row_fn, ln_g, ln_b, alpha, tm):
    m, d = x.shape
    return pl.pallas_call(
        functools.partial(_proj_ln_kernel, alpha=alpha),
        grid=(m // tm,),
        in_specs=[
            pl.BlockSpec((N_HEADS, tm, HEAD_DIM), lambda i: (0, i, 0)),
            pl.BlockSpec((d, d), lambda i: (0, 0)),
            pl.BlockSpec((tm, d), lambda i: (i, 0)),
            _ada_spec(layer, 2, row_fn, d),
            _vec_spec(d),
            _vec_spec(d),
        ],
        out_specs=pl.BlockSpec((tm, d), lambda i: (i, 0)),
        out_shape=jax.ShapeDtypeStruct((m, d), F32),
        scratch_shapes=[pltpu.VMEM((tm, d), BF16)],
        compiler_params=_cparams("parallel"),
        name="out_proj_norm",
    )(a, w_o, x, ada, ln_g, ln_b)


def _pool_kernel(x_ref, xp_ref, xn_ref, sh_ref, sc_ref, g_ref, w_ref, ps_ref, lng_ref, lnb_ref, o_ref, ext_ref,
                 *, alpha, seq_len):
    tm, d = x_ref.shape
    gw = d // len(POOL_WINDOWS)
    start = pl.program_id(0) * tm
    pos0 = start % seq_len
    sc1 = 1.0 + sc_ref[...]
    sh = sh_ref[...]
    h = x_ref[...] * sc1 + sh
    ext_ref[0:POOL_HALO, :] = jnp.where(pos0 > 0, xp_ref[...] * sc1 + sh, 0.0)
    ext_ref[POOL_HALO:POOL_HALO + tm, :] = h
    ext_ref[POOL_HALO + tm:, :] = jnp.where(pos0 + tm < seq_len, xn_ref[...] * sc1 + sh, 0.0)
    pos = pos0 + lax.broadcasted_iota(jnp.int32, (tm, gw), 0)
    ys = []
    for gi, w in enumerate(POOL_WINDOWS):
        cols = slice(gi * gw, (gi + 1) * gw)
        tot = functools.reduce(
            jnp.add, [ext_ref[POOL_HALO + o:POOL_HALO + o + tm, cols] for o in range(-(w // 2), w - w // 2)])
        cnt = jnp.minimum(pos - w // 2 + w, seq_len) - jnp.maximum(pos - w // 2, 0)
        pooled = tot / cnt.astype(F32) - h[:, cols]
        ys.append(jnp.dot(pooled.astype(BF16), w_ref[gi], preferred_element_type=F32))
    y = jnp.concatenate(ys, axis=1) * ps_ref[...]
    o_ref[...] = _layer_norm(alpha * x_ref[...] + g_ref[...] * y, lng_ref[...], lnb_ref[...])


def _pool_mix(x, ada, layer, row_fn, w_pool, pool_scale, ln_g, ln_b, alpha, tm, seq_len):
    m, d = x.shape
    hb = tm // POOL_HALO
    n_halo_blocks = m // POOL_HALO
    gw = d // len(POOL_WINDOWS)
    return pl.pallas_call(
        functools.partial(_pool_kernel, alpha=alpha, seq_len=seq_len),
        grid=(m // tm,),
        in_specs=[
            pl.BlockSpec((tm, d), lambda i: (i, 0)),
            pl.BlockSpec((POOL_HALO, d), lambda i: (jnp.maximum(i * hb - 1, 0), 0)),
            pl.BlockSpec((POOL_HALO, d), lambda i: (jnp.minimum((i + 1) * hb, n_halo_blocks - 1), 0)),
            _ada_spec(layer, 0, row_fn, d),
            _ada_spec(layer, 1, row_fn, d),
            _ada_spec(layer, 2, row_fn, d),
            pl.BlockSpec((len(POOL_WINDOWS), gw, gw), lambda i: (0, 0, 0)),
            _vec_spec(d),
            _vec_spec(d),
            _vec_spec(d),
        ],
        out_specs=pl.BlockSpec((tm, d), lambda i: (i, 0)),
        out_shape=jax.ShapeDtypeStruct((m, d), F32),
        scratch_shapes=[pltpu.VMEM((tm + 2 * POOL_HALO, d), F32)],
        compiler_params=_cparams("parallel"),
        name="pool_mix_norm",
    )(x, x, x, ada, ada, ada, w_pool, pool_scale, ln_g, ln_b)


def _ffn_kernel(x_ref, sh_ref, sc_ref, g_ref, wg_ref, wu_ref, wo_ref, lng_ref, lnb_ref, o_ref, h_ref, *, alpha):
    f = pl.program_id(1)

    @pl.when(f == 0)
    def _():
        h_ref[...] = (x_ref[...] * (1.0 + sc_ref[...]) + sh_ref[...]).astype(BF16)

    h = h_ref[...]
    gate = jnp.dot(h, wg_ref[...], preferred_element_type=F32)
    up = jnp.dot(h, wu_ref[...], preferred_element_type=F32)
    part = jnp.dot((_silu(gate) * up).astype(BF16), wo_ref[...], preferred_element_type=F32)

    @pl.when(f == 0)
    def _():
        o_ref[...] = part

    @pl.when(f > 0)
    def _():
        o_ref[...] += part

    @pl.when(f == pl.num_programs(1) - 1)
    def _():
        o_ref[...] = _layer_norm(alpha * x_ref[...] + g_ref[...] * o_ref[...], lng_ref[...], lnb_ref[...])


def _ffn(x, ada, layer, row_fn, w_in, w_out, ln_g, ln_b, alpha, tm, tf=512):
    m, d = x.shape
    d_ff = w_out.shape[0]
    nf = d_ff // tf
    return pl.pallas_call(
        functools.partial(_ffn_kernel, alpha=alpha),
        grid=(m // tm, nf),
        in_specs=[
            pl.BlockSpec((tm, d), lambda i, f: (i, 0)),
            _ada_spec(layer, 3, row_fn, d),
            _ada_spec(layer, 4, row_fn, d),
            _ada_spec(layer, 5, row_fn, d),
            pl.BlockSpec((d, tf), lambda i, f: (0, f)),
            pl.BlockSpec((d, tf), lambda i, f: (0, nf + f)),
            pl.BlockSpec((tf, d), lambda i, f: (f, 0)),
            _vec_spec(d),
            _vec_spec(d),
        ],
        out_specs=pl.BlockSpec((tm, d), lambda i, f: (i, 0)),
        out_shape=jax.ShapeDtypeStruct((m, d), F32),
        scratch_shapes=[pltpu.VMEM((tm, d), BF16)],
        compiler_params=_cparams("parallel", "arbitrary"),
        name="ffn_norm",
    )(x, ada, ada, ada, w_in, w_in, w_out, ln_g, ln_b)


def kernel(x, c, ctx, c_ctx, ada_w, ada_b, ln_mix_g, ln_mix_b, ln_ffn_g, ln_ffn_b,
           na_w_qkv, na_w_o, na_rpb, pool_w, pool_scale, ffn_w_in, ffn_w_out):
    batch, seq, d = x.shape
    n_ctx = ctx.shape[1]
    depth = ada_w.shape[0]
    n_mixers = 2
    assert seq % Q_TOK == 0 and seq == GRID_W * GRID_W and batch < COND_ROWS
    alpha = (2 * depth) ** 0.25
    last_na = max(i for i in range(depth) if i % n_mixers == 0)

    cond = jnp.concatenate([c, c_ctx[None], jnp.zeros((COND_ROWS - batch - 1, d), F32)], axis=0)
    ada = _ada_params(cond, ada_w, ada_b).reshape(depth, COND_ROWS, N_ADA, 1, d)

    tm = 512
    lat_row = lambda i: i // (seq // tm)
    lat_row_qkv = lambda i: i // (seq // 1024)
    ctx_row = lambda i: batch
    ctx_tm = min(tm, batch * n_ctx)

    qscale = jnp.concatenate([jnp.full((d,), HEAD_DIM ** -0.5 * LOG2E, F32), jnp.ones((2 * d,), F32)])
    w_qkv = (na_w_qkv * qscale).astype(BF16)
    w_o = na_w_o.astype(BF16)
    w_pool = pool_w.astype(BF16)
    w_in = ffn_w_in.astype(BF16)
    w_out = ffn_w_out.astype(BF16)

    xl = x.reshape(batch * seq, d)
    xc = ctx.reshape(batch * n_ctx, d)
    for i in range(depth):
        use_na = i % n_mixers == 0
        j = i // n_mixers
        update_ctx = i < last_na
        lmg, lmb = ln_mix_g[i].reshape(1, d), ln_mix_b[i].reshape(1, d)
        lfg, lfb = ln_ffn_g[i].reshape(1, d), ln_ffn_b[i].reshape(1, d)
        if use_na:
            qk, vt = _qkv_proj(xl, ada, i, lat_row_qkv, w_qkv[j], 1024)
            qk_c, vt_c = _qkv_proj(xc, ada, i, ctx_row, w_qkv[j], batch * n_ctx)
            att = _neighbourhood_attention(qk, vt, qk_c, vt_c, _bias_tiles(na_rpb[j]), batch)
            xl = _proj_ln(att, w_o[j], xl, ada, i, lat_row, lmg, lmb, alpha, tm)
            if update_ctx:
                att_c = _context_attention(qk_c, vt_c, batch)
                xc = _proj_ln(att_c, w_o[j], xc, ada, i, ctx_row, lmg, lmb, alpha, ctx_tm)
        else:
            ps = pool_scale[j].reshape(1, d)
            xl = _pool_mix(xl, ada, i, lat_row, w_pool[j], ps, lmg, lmb, alpha, tm, seq)
            if update_ctx:
                xc = _pool_mix(xc, ada, i, ctx_row, w_pool[j], ps, lmg, lmb, alpha, n_ctx, n_ctx)
        xl = _ffn(xl, ada, i, lat_row, w_in[i], w_out[i], lfg, lfb, alpha, tm)
        if update_ctx:
            xc = _ffn(xc, ada, i, ctx_row, w_in[i], w_out[i], lfg, lfb, alpha, ctx_tm)
    return xl.reshape(batch, seq, d)
```

```python
import functools
import math

import numpy as np
import jax
import jax.numpy as jnp
from jax import lax
from jax.experimental import pallas as pl
from jax.experimental.pallas import tpu as pltpu

F32 = jnp.float32
BF16 = jnp.bfloat16

GRID_W = 64
N_HEADS = 16
HEAD_DIM = 128
WIN_ROWS = 8
WIN_COLS = 16
POOL_WINDOWS = (2, 4, 8, 16)
N_ADA = 6
LN_EPS = 1e-5
LOG2E = math.log2(math.e)

Q_ROWS = 4
K_ROWS = Q_ROWS + WIN_ROWS
Q_TOK = Q_ROWS * GRID_W
K_TOK = K_ROWS * GRID_W
N_BIAS_TILES = 18
NA_SLOTS = 7
POOL_HALO = 8
COND_ROWS = 8
QKV_HEADS_PER_TILE = 2

VMEM_LIMIT = 56 * 1024 * 1024


def _cparams(*sem):
    return pltpu.CompilerParams(dimension_semantics=sem, vmem_limit_bytes=VMEM_LIMIT)


def _layer_norm(v, g, b):
    mu = jnp.mean(v, axis=-1, keepdims=True)
    d = v - mu
    var = jnp.mean(d * d, axis=-1, keepdims=True)
    return d * lax.rsqrt(var + LN_EPS) * g + b


def _silu(v):
    return v * jax.nn.sigmoid(v)


def _ada_kernel(cond_ref, w_ref, b_ref, o_ref):
    s = _silu(cond_ref[...]).astype(BF16)
    o_ref[...] = jnp.dot(s, w_ref[...].astype(BF16), preferred_element_type=F32) + b_ref[...]


def _ada_params(cond, ada_w, ada_b, tn=1024):
    depth, d, n = ada_w.shape
    return pl.pallas_call(
        _ada_kernel,
        grid=(depth, n // tn),
        in_specs=[
            pl.BlockSpec((COND_ROWS, d), lambda l, j: (0, 0)),
            pl.BlockSpec((None, d, tn), lambda l, j: (l, 0, j)),
            pl.BlockSpec((None, 1, tn), lambda l, j: (l, 0, j)),
        ],
        out_specs=pl.BlockSpec((None, COND_ROWS, tn), lambda l, j: (l, 0, j)),
        out_shape=jax.ShapeDtypeStruct((depth, COND_ROWS, n), F32),
        compiler_params=_cparams("parallel", "parallel"),
        name="ada_params",
    )(cond, ada_w, ada_b.reshape(depth, 1, n))


def _ada_spec(layer, which, row_fn, d):
    return pl.BlockSpec((None, None, None, 1, d), lambda i, *_: (layer, row_fn(i), which, 0, 0))


def _vec_spec(layer, d):
    return pl.BlockSpec((None, 1, d), lambda i, *_: (layer, 0, 0))


def _qkv_kernel(x_ref, sh_ref, sc_ref, w_ref, q_ref, k_ref, vt_ref, h_ref):
    @pl.when(pl.program_id(1) == 0)
    def _():
        h_ref[...] = (x_ref[...] * (1.0 + sc_ref[...]) + sh_ref[...]).astype(BF16)

    res = jnp.dot(h_ref[...], w_ref[...], preferred_element_type=F32)
    hpt = q_ref.shape[0]
    for hh in range(hpt):
        q_ref[hh] = res[:, hh * HEAD_DIM:(hh + 1) * HEAD_DIM].astype(BF16)
        k_ref[hh] = res[:, (hpt + hh) * HEAD_DIM:(hpt + hh + 1) * HEAD_DIM].astype(BF16)
        vt_ref[hh] = res[:, (2 * hpt + hh) * HEAD_DIM:(2 * hpt + hh + 1) * HEAD_DIM].T.astype(BF16)


def _qkv_proj(x, ada, layer, row_fn, w_qkv, j_na, tm):
    m, d = x.shape
    hpt = QKV_HEADS_PER_TILE
    tn = 3 * hpt * HEAD_DIM
    return pl.pallas_call(
        _qkv_kernel,
        grid=(m // tm, N_HEADS // hpt),
        in_specs=[
            pl.BlockSpec((tm, d), lambda i, j: (i, 0)),
            _ada_spec(layer, 0, row_fn, d),
            _ada_spec(layer, 1, row_fn, d),
            pl.BlockSpec((None, d, tn), lambda i, j: (j_na, 0, j)),
        ],
        out_specs=[
            pl.BlockSpec((hpt, tm, HEAD_DIM), lambda i, j: (j, i, 0)),
            pl.BlockSpec((hpt, tm, HEAD_DIM), lambda i, j: (j, i, 0)),
            pl.BlockSpec((hpt, HEAD_DIM, tm), lambda i, j: (j, 0, i)),
        ],
        out_shape=[
            jax.ShapeDtypeStruct((N_HEADS, m, HEAD_DIM), BF16),
            jax.ShapeDtypeStruct((N_HEADS, m, HEAD_DIM), BF16),
            jax.ShapeDtypeStruct((N_HEADS, HEAD_DIM, m), BF16),
        ],
        scratch_shapes=[pltpu.VMEM((tm, d), BF16)],
        compiler_params=_cparams("parallel", "arbitrary"),
        name="qkv_proj",
    )(x, ada, ada, w_qkv)


def _bias_tiles(rpb):
    n_dr, n_dc = rpb.shape[1], rpb.shape[2]
    period = 2 * GRID_W + 1
    rev = jnp.pad(rpb[..., ::-1], ((0, 0), (0, 0), (0, period - n_dc)))
    skew = jnp.tile(rev, (1, 1, GRID_W))[..., :GRID_W * (period - 1)].reshape(-1, n_dr, GRID_W, period - 1)
    toep = skew[..., WIN_COLS - 1:WIN_COLS - 1 + GRID_W]
    first = jnp.concatenate([toep, toep[:, n_dr - 1:], toep[:, 3:4], toep[:, 11:12]], axis=1)
    second = jnp.concatenate([toep[:, :1], toep, toep[:, 2:3], toep[:, 10:11]], axis=1)
    vals = jnp.concatenate([first, second], axis=-1) * LOG2E

    e = np.arange(N_BIAS_TILES)[:, None, None]
    ck = np.arange(GRID_W)[None, :, None]
    lane = np.arange(2 * GRID_W)[None, None, :]
    half, cq = lane // GRID_W, lane % GRID_W
    d = np.where(e == 16, -4, np.where(e == 17, 4, e - (WIN_ROWS - 1)))
    dr = d - half
    row_ok = (np.abs(dr) <= WIN_ROWS - 1) & ((e != 16) | (half == 0)) & ((e != 17) | (half == 1))
    ws = np.clip(cq - WIN_COLS // 2, 0, GRID_W - WIN_COLS)
    ok = row_ok & (ck >= ws) & (ck < ws + WIN_COLS)
    return jnp.where(jnp.asarray(ok)[None], vals, -jnp.inf).astype(F32)


def _tile_entry(d):
    return d + WIN_ROWS - 1


_TILES_TOP = [[(j, _tile_entry(j - 2 * u)) for j in range(0, WIN_ROWS)] for u in range(Q_ROWS // 2)]
_TILES_BOT = [[(j, _tile_entry(j - WIN_ROWS - 2 * u)) for j in range(K_ROWS - WIN_ROWS, K_ROWS)]
              for u in range(Q_ROWS // 2)]
_TILES_MID = [[(j, 16 if j - 2 * u - 4 == -4 else 17 if j - 2 * u - 4 == 4 else _tile_entry(j - 2 * u - 4))
               for j in range(2 * u, 2 * u + WIN_ROWS + 1)] for u in range(Q_ROWS // 2)]


def _na_kernel(q_ref, k_ref, vt_ref, kc_ref, vct_ref, tab_ref, o_ref, st_ref, pt_ref, *, n_blocks, n_ctx):
    lanes = 2 * GRID_W
    n_ctx_tiles = n_ctx // GRID_W
    zero_tile = jnp.zeros((GRID_W, lanes), BF16)

    def zero_unused(slot, tiles):
        for u in range(Q_ROWS // 2):
            used = dict(tiles[u])
            for j in range(K_ROWS):
                if j not in used:
                    pt_ref[slot, j * GRID_W:(j + 1) * GRID_W, u * lanes:(u + 1) * lanes] = zero_tile

    def block(slot, q0, k0, tiles):
        q = q_ref[pl.ds(q0, Q_TOK), :]
        nt = (((1,), (1,)), ((), ()))
        st_ref[slot, 0:K_TOK, :] = lax.dot_general(k_ref[pl.ds(k0, K_TOK), :], q, nt, preferred_element_type=F32)
        st_ref[slot, K_TOK:, :] = lax.dot_general(kc_ref[...], q, nt, preferred_element_type=F32)
        inv_l = []
        for u in range(Q_ROWS // 2):
            cols = slice(u * lanes, (u + 1) * lanes)
            sc = [st_ref[slot, j * GRID_W:(j + 1) * GRID_W, cols] + tab_ref[e] for j, e in tiles[u]]
            sc += [st_ref[slot, K_TOK + t * GRID_W:K_TOK + (t + 1) * GRID_W, cols] for t in range(n_ctx_tiles)]
            m = jnp.max(functools.reduce(jnp.maximum, sc), axis=0, keepdims=True)
            ps = [jnp.exp2(s - m) for s in sc]
            inv_l.append(1.0 / jnp.sum(functools.reduce(jnp.add, ps), axis=0, keepdims=True))
            rows = [j for j, _ in tiles[u]] + [K_ROWS + t for t in range(n_ctx_tiles)]
            for j, p in zip(rows, ps):
                pt_ref[slot, j * GRID_W:(j + 1) * GRID_W, cols] = p.astype(BF16)
        ot = jnp.dot(vt_ref[:, pl.ds(k0, K_TOK)], pt_ref[slot, 0:K_TOK, :], preferred_element_type=F32)
        ot += jnp.dot(vct_ref[...], pt_ref[slot, K_TOK:, :], preferred_element_type=F32)
        ot = ot * jnp.concatenate(inv_l, axis=1)
        o_ref[pl.ds(q0, Q_TOK), :] = ot.T.astype(BF16)

    zero_unused(0, _TILES_TOP)
    block(0, 0, 0, _TILES_TOP)
    for slot in range(NA_SLOTS):
        zero_unused(slot, _TILES_MID)

    def body(i, carry):
        for slot in range(NA_SLOTS):
            blk = 1 + i * NA_SLOTS + slot
            q0 = pl.multiple_of(blk * Q_TOK, Q_TOK)
            k0 = pl.multiple_of((blk - 1) * Q_TOK, Q_TOK)
            block(slot, q0, k0, _TILES_MID)
        return carry

    lax.fori_loop(0, (n_blocks - 2) // NA_SLOTS, body, 0)
    zero_unused(0, _TILES_BOT)
    block(0, (n_blocks - 1) * Q_TOK, n_blocks * Q_TOK - K_TOK, _TILES_BOT)


def _neighbourhood_attention(q, k, vt, k_c, vt_c, tab, batch):
    m = q.shape[1]
    seq = m // batch
    n_ctx = k_c.shape[1] // batch
    n_blocks = seq // Q_TOK
    assert (n_blocks - 2) % NA_SLOTS == 0
    kern = functools.partial(_na_kernel, n_blocks=n_blocks, n_ctx=n_ctx)
    return pl.pallas_call(
        kern,
        grid=(batch, N_HEADS),
        in_specs=[
            pl.BlockSpec((None, seq, HEAD_DIM), lambda b, h: (h, b, 0)),
            pl.BlockSpec((None, seq, HEAD_DIM), lambda b, h: (h, b, 0)),
            pl.BlockSpec((None, HEAD_DIM, seq), lambda b, h: (h, 0, b)),
            pl.BlockSpec((None, n_ctx, HEAD_DIM), lambda b, h: (h, b, 0)),
            pl.BlockSpec((None, HEAD_DIM, n_ctx), lambda b, h: (h, 0, b)),
            pl.BlockSpec((None, N_BIAS_TILES, GRID_W, 2 * GRID_W), lambda b, h: (h, 0, 0, 0)),
        ],
        out_specs=pl.BlockSpec((None, seq, HEAD_DIM), lambda b, h: (h, b, 0)),
        out_shape=jax.ShapeDtypeStruct((N_HEADS, m, HEAD_DIM), BF16),
        scratch_shapes=[
            pltpu.VMEM((NA_SLOTS, K_TOK + n_ctx, Q_TOK), F32),
            pltpu.VMEM((NA_SLOTS, K_TOK + n_ctx, Q_TOK), BF16),
        ],
        compiler_params=_cparams("parallel", "parallel"),
        name="neighbourhood_attention",
    )(q, k, vt, k_c, vt_c, tab)


def _ctx_attn_kernel(q_ref, k_ref, vt_ref, o_ref):
    nt = (((1,), (1,)), ((), ()))
    st = lax.dot_general(k_ref[...], q_ref[...], nt, preferred_element_type=F32)
    p = jnp.exp2(st - jnp.max(st, axis=0, keepdims=True))
    inv_l = 1.0 / jnp.sum(p, axis=0, keepdims=True)
    ot = jnp.dot(vt_ref[...], p.astype(BF16), preferred_element_type=F32) * inv_l
    o_ref[...] = ot.T.astype(BF16)


def _context_attention(q_c, k_c, vt_c, batch):
    m = q_c.shape[1]
    n_ctx = m // batch
    return pl.pallas_call(
        _ctx_attn_kernel,
        grid=(batch, N_HEADS),
        in_specs=[
            pl.BlockSpec((None, n_ctx, HEAD_DIM), lambda b, h: (h, b, 0)),
            pl.BlockSpec((None, n_ctx, HEAD_DIM), lambda b, h: (h, b, 0)),
            pl.BlockSpec((None, HEAD_DIM, n_ctx), lambda b, h: (h, 0, b)),
        ],
        out_specs=pl.BlockSpec((None, n_ctx, HEAD_DIM), lambda b, h: (h, b, 0)),
        out_shape=jax.ShapeDtypeStruct((N_HEADS, m, HEAD_DIM), BF16),
        compiler_params=_cparams("parallel", "parallel"),
        name="context_attention",
    )(q_c, k_c, vt_c)


def _proj_ln_kernel(a_ref, w_ref, x_ref, g_ref, lng_ref, lnb_ref, o_ref, acat_ref, *, alpha):
    for h in range(N_HEADS):
        acat_ref[:, h * HEAD_DIM:(h + 1) * HEAD_DIM] = a_ref[h]
    y = jnp.dot(acat_ref[...], w_ref[...], preferred_element_type=F32)
    o_ref[...] = _layer_norm(alpha * x_ref[...] + g_ref[...] * y, lng_ref[...], lnb_ref[...])


def _proj_ln(a, w_o, j_na, x, ada, layer, row_fn, ln_g, ln_b, alpha, tm):
    m, d = x.shape
    return pl.pallas_call(
        functools.partial(_proj_ln_kernel, alpha=alpha),
        grid=(m // tm,),
        in_specs=[
            pl.BlockSpec((N_HEADS, tm, HEAD_DIM), lambda i: (0, i, 0)),
            pl.BlockSpec((None, d, d), lambda i: (j_na, 0, 0)),
            pl.BlockSpec((tm, d), lambda i: (i, 0)),
            _ada_spec(layer, 2, row_fn, d),
            _vec_spec(layer, d),
            _vec_spec(layer, d),
        ],
        out_specs=pl.BlockSpec((tm, d), lambda i: (i, 0)),
        out_shape=jax.ShapeDtypeStruct((m, d), F32),
        scratch_shapes=[pltpu.VMEM((tm, d), BF16)],
        compiler_params=_cparams("parallel"),
        name="out_proj_norm",
    )(a, w_o, x, ada, ln_g, ln_b)


def _pool_kernel(x_ref, xp_ref, xn_ref, sh_ref, sc_ref, g_ref, w_ref, ps_ref, lng_ref, lnb_ref, o_ref, ext_ref,
                 *, alpha, seq_len):
    tm, d = x_ref.shape
    gw = d // len(POOL_WINDOWS)
    start = pl.program_id(0) * tm
    pos0 = start % seq_len
    sc1 = 1.0 + sc_ref[...]
    sh = sh_ref[...]
    h = x_ref[...] * sc1 + sh
    ext_ref[0:POOL_HALO, :] = jnp.where(pos0 > 0, xp_ref[...] * sc1 + sh, 0.0)
    ext_ref[POOL_HALO:POOL_HALO + tm, :] = h
    ext_ref[POOL_HALO + tm:, :] = jnp.where(pos0 + tm < seq_len, xn_ref[...] * sc1 + sh, 0.0)
    pos = pos0 + lax.broadcasted_iota(jnp.int32, (tm, gw), 0)
    ys = []
    for gi, w in enumerate(POOL_WINDOWS):
        cols = slice(gi * gw, (gi + 1) * gw)
        tot = functools.reduce(
            jnp.add, [ext_ref[POOL_HALO + o:POOL_HALO + o + tm, cols] for o in range(-(w // 2), w - w // 2)])
        cnt = jnp.minimum(pos - w // 2 + w, seq_len) - jnp.maximum(pos - w // 2, 0)
        pooled = tot / cnt.astype(F32) - h[:, cols]
        ys.append(jnp.dot(pooled.astype(BF16), w_ref[gi], preferred_element_type=F32))
    y = jnp.concatenate(ys, axis=1) * ps_ref[...]
    o_ref[...] = _layer_norm(alpha * x_ref[...] + g_ref[...] * y, lng_ref[...], lnb_ref[...])


def _pool_mix(x, ada, layer, row_fn, w_pool, pool_scale, j_pool, ln_g, ln_b, alpha, tm, seq_len):
    m, d = x.shape
    hb = tm // POOL_HALO
    n_halo_blocks = m // POOL_HALO
    gw = d // len(POOL_WINDOWS)
    return pl.pallas_call(
        functools.partial(_pool_kernel, alpha=alpha, seq_len=seq_len),
        grid=(m // tm,),
        in_specs=[
            pl.BlockSpec((tm, d), lambda i: (i, 0)),
            pl.BlockSpec((POOL_HALO, d), lambda i: (jnp.maximum(i * hb - 1, 0), 0)),
            pl.BlockSpec((POOL_HALO, d), lambda i: (jnp.minimum((i + 1) * hb, n_halo_blocks - 1), 0)),
            _ada_spec(layer, 0, row_fn, d),
            _ada_spec(layer, 1, row_fn, d),
            _ada_spec(layer, 2, row_fn, d),
            pl.BlockSpec((None, len(POOL_WINDOWS), gw, gw), lambda i: (j_pool, 0, 0, 0)),
            _vec_spec(j_pool, d),
            _vec_spec(layer, d),
            _vec_spec(layer, d),
        ],
        out_specs=pl.BlockSpec((tm, d), lambda i: (i, 0)),
        out_shape=jax.ShapeDtypeStruct((m, d), F32),
        scratch_shapes=[pltpu.VMEM((tm + 2 * POOL_HALO, d), F32)],
        compiler_params=_cparams("parallel"),
        name="pool_mix_norm",
    )(x, x, x, ada, ada, ada, w_pool, pool_scale, ln_g, ln_b)


def _ffn_kernel(x_ref, sh_ref, sc_ref, g_ref, wg_ref, wu_ref, wo_ref, lng_ref, lnb_ref, o_ref, h_ref, *, alpha):
    f = pl.program_id(1)

    @pl.when(f == 0)
    def _():
        h_ref[...] = (x_ref[...] * (1.0 + sc_ref[...]) + sh_ref[...]).astype(BF16)
        o_ref[...] = jnp.zeros_like(o_ref)

    h = h_ref[...]
    gate = jnp.dot(h, wg_ref[...], preferred_element_type=F32)
    up = jnp.dot(h, wu_ref[...], preferred_element_type=F32)
    o_ref[...] += jnp.dot((_silu(gate) * up).astype(BF16), wo_ref[...], preferred_element_type=F32)

    @pl.when(f == pl.num_programs(1) - 1)
    def _():
        o_ref[...] = _layer_norm(alpha * x_ref[...] + g_ref[...] * o_ref[...], lng_ref[...], lnb_ref[...])


def _ffn(x, ada, layer, row_fn, w_in, w_out, ln_g, ln_b, alpha, tm, tf=512):
    m, d = x.shape
    d_ff = w_out.shape[1]
    nf = d_ff // tf
    return pl.pallas_call(
        functools.partial(_ffn_kernel, alpha=alpha),
        grid=(m // tm, nf),
        in_specs=[
            pl.BlockSpec((tm, d), lambda i, f: (i, 0)),
            _ada_spec(layer, 3, row_fn, d),
            _ada_spec(layer, 4, row_fn, d),
            _ada_spec(layer, 5, row_fn, d),
            pl.BlockSpec((None, d, tf), lambda i, f: (layer, 0, f)),
            pl.BlockSpec((None, d, tf), lambda i, f: (layer, 0, nf + f)),
            pl.BlockSpec((None, tf, d), lambda i, f: (layer, f, 0)),
            _vec_spec(layer, d),
            _vec_spec(layer, d),
        ],
        out_specs=pl.BlockSpec((tm, d), lambda i, f: (i, 0)),
        out_shape=jax.ShapeDtypeStruct((m, d), F32),
        scratch_shapes=[pltpu.VMEM((tm, d), BF16)],
        compiler_params=_cparams("parallel", "arbitrary"),
        name="ffn_norm",
    )(x, ada, ada, ada, w_in, w_in, w_out, ln_g, ln_b)


def kernel(x, c, ctx, c_ctx, ada_w, ada_b, ln_mix_g, ln_mix_b, ln_ffn_g, ln_ffn_b,
           na_w_qkv, na_w_o, na_rpb, pool_w, pool_scale, ffn_w_in, ffn_w_out):
    batch, seq, d = x.shape
    n_ctx = ctx.shape[1]
    depth = ada_w.shape[0]
    n_mixers = 2
    assert seq % Q_TOK == 0 and seq == GRID_W * GRID_W and batch < COND_ROWS
    alpha = (2 * depth) ** 0.25
    last_na = max(i for i in range(depth) if i % n_mixers == 0)

    cond = jnp.concatenate([c, c_ctx[None], jnp.zeros((COND_ROWS - batch - 1, d), F32)], axis=0)
    ada = _ada_params(cond, ada_w, ada_b).reshape(depth, COND_ROWS, N_ADA, 1, d)

    tm = 512
    tm_qkv = 1024
    lat_row = lambda i: i // (seq // tm)
    lat_row_qkv = lambda i: i // (seq // tm_qkv)
    ctx_row = lambda i: batch
    ctx_tm = min(tm, batch * n_ctx)

    qscale = jnp.concatenate([jnp.full((d,), HEAD_DIM ** -0.5 * LOG2E, F32), jnp.ones((2 * d,), F32)])
    n_groups = N_HEADS // QKV_HEADS_PER_TILE
    w_qkv = (na_w_qkv * qscale).astype(BF16).reshape(-1, d, 3, n_groups, QKV_HEADS_PER_TILE * HEAD_DIM)
    w_qkv = w_qkv.transpose(0, 1, 3, 2, 4).reshape(-1, d, 3 * d)
    w_o = na_w_o.astype(BF16)
    w_pool = pool_w.astype(BF16)
    w_in = ffn_w_in.astype(BF16)
    w_out = ffn_w_out.astype(BF16)
    lmg, lmb = ln_mix_g.reshape(depth, 1, d), ln_mix_b.reshape(depth, 1, d)
    lfg, lfb = ln_ffn_g.reshape(depth, 1, d), ln_ffn_b.reshape(depth, 1, d)
    ps = pool_scale.reshape(-1, 1, d)

    xl = x.reshape(batch * seq, d)
    xc = ctx.reshape(batch * n_ctx, d)
    for i in range(depth):
        use_na = i % n_mixers == 0
        j = i // n_mixers
        update_ctx = i < last_na
        if use_na:
            q, k, vt = _qkv_proj(xl, ada, i, lat_row_qkv, w_qkv, j, tm_qkv)
            q_c, k_c, vt_c = _qkv_proj(xc, ada, i, ctx_row, w_qkv, j, batch * n_ctx)
            att = _neighbourhood_attention(q, k, vt, k_c, vt_c, _bias_tiles(na_rpb[j]), batch)
            xl = _proj_ln(att, w_o, j, xl, ada, i, lat_row, lmg, lmb, alpha, tm)
            if update_ctx:
                att_c = _context_attention(q_c, k_c, vt_c, batch)
                xc = _proj_ln(att_c, w_o, j, xc, ada, i, ctx_row, lmg, lmb, alpha, ctx_tm)
        else:
            xl = _pool_mix(xl, ada, i, lat_row, w_pool, ps, j, lmg, lmb, alpha, tm, seq)
            if update_ctx:
                xc = _pool_mix(xc, ada, i, ctx_row, w_pool, ps, j, lmg, lmb, alpha, n_ctx, n_ctx)
        xl = _ffn(xl, ada, i, lat_row, w_in, w_out, lfg, lfb, alpha, tm)
        if update_ctx:
            xc = _ffn(xc, ada, i, ctx_row, w_in, w_out, lfg, lfb, alpha, ctx_tm)
    return xl.reshape(batch, seq, d)
```

```python
import functools
import math

import numpy as np
import jax
import jax.numpy as jnp
from jax import lax
from jax.experimental import pallas as pl
from jax.experimental.pallas import tpu as pltpu

F32 = jnp.float32
BF16 = jnp.bfloat16

GRID_W = 64
N_HEADS = 16
HEAD_DIM = 128
WIN_ROWS = 8
WIN_COLS = 16
POOL_WINDOWS = (2, 4, 8, 16)
N_ADA = 6
LN_EPS = 1e-5
LOG2E = math.log2(math.e)

Q_ROWS = 4
K_ROWS = Q_ROWS + WIN_ROWS
Q_TOK = Q_ROWS * GRID_W
K_TOK = K_ROWS * GRID_W
N_BIAS_TILES = 18
POOL_HALO = 8
COND_ROWS = 8
ROW_CHUNK = 128
QKV_HEADS_PER_TILE = 4

VMEM_LIMIT = 58 * 1024 * 1024


def _cparams(*sem):
    return pltpu.CompilerParams(dimension_semantics=sem, vmem_limit_bytes=VMEM_LIMIT)


def _layer_norm(v, g, b):
    mu = jnp.mean(v, axis=-1, keepdims=True)
    d = v - mu
    var = jnp.mean(d * d, axis=-1, keepdims=True)
    return d * lax.rsqrt(var + LN_EPS) * g + b


def _silu(v):
    return v * jax.nn.sigmoid(v)


def _for_row_chunks(n_rows, fn):
    def body(c, carry):
        fn(pl.ds(pl.multiple_of(c * ROW_CHUNK, ROW_CHUNK), ROW_CHUNK))
        return carry
    lax.fori_loop(0, n_rows // ROW_CHUNK, body, 0)


def _modulate_rows(x_ref, sh_ref, sc_ref, h_ref):
    sc1 = 1.0 + sc_ref[...]
    sh = sh_ref[...]

    def chunk(rows):
        h_ref[rows, :] = (x_ref[rows, :] * sc1 + sh).astype(BF16)
    _for_row_chunks(x_ref.shape[0], chunk)


def _ada_kernel(cond_ref, w_ref, b_ref, o_ref):
    s = _silu(cond_ref[...]).astype(BF16)
    o_ref[...] = jnp.dot(s, w_ref[...].astype(BF16), preferred_element_type=F32) + b_ref[...]


def _ada_params(cond, ada_w, ada_b, tn=1024):
    depth, d, n = ada_w.shape
    return pl.pallas_call(
        _ada_kernel,
        grid=(depth, n // tn),
        in_specs=[
            pl.BlockSpec((COND_ROWS, d), lambda l, j: (0, 0)),
            pl.BlockSpec((None, d, tn), lambda l, j: (l, 0, j)),
            pl.BlockSpec((None, 1, tn), lambda l, j: (l, 0, j)),
        ],
        out_specs=pl.BlockSpec((None, COND_ROWS, tn), lambda l, j: (l, 0, j)),
        out_shape=jax.ShapeDtypeStruct((depth, COND_ROWS, n), F32),
        compiler_params=_cparams("parallel", "parallel"),
        name="ada_params",
    )(cond, ada_w, ada_b.reshape(depth, 1, n))


def _ada_spec(layer, which, row_fn, d):
    return pl.BlockSpec((None, None, None, 1, d), lambda i, *_: (layer, row_fn(i), which, 0, 0))


def _vec_spec(layer, d):
    return pl.BlockSpec((None, 1, d), lambda i, *_: (layer, 0, 0))


def _qkv_kernel(x_ref, sh_ref, sc_ref, w_ref, q_ref, k_ref, vt_ref, h_ref):
    @pl.when(pl.program_id(1) == 0)
    def _():
        _modulate_rows(x_ref, sh_ref, sc_ref, h_ref)

    res = jnp.dot(h_ref[...], w_ref[...], preferred_element_type=F32)
    hpt = q_ref.shape[0]
    for hh in range(hpt):
        q_ref[hh] = res[:, hh * HEAD_DIM:(hh + 1) * HEAD_DIM].astype(BF16)
        k_ref[hh] = res[:, (hpt + hh) * HEAD_DIM:(hpt + hh + 1) * HEAD_DIM].astype(BF16)
        vt_ref[hh] = res[:, (2 * hpt + hh) * HEAD_DIM:(2 * hpt + hh + 1) * HEAD_DIM].T.astype(BF16)


def _qkv_proj(x, ada, layer, row_fn, w_qkv, j_na, tm):
    m, d = x.shape
    hpt = QKV_HEADS_PER_TILE
    tn = 3 * hpt * HEAD_DIM
    return pl.pallas_call(
        _qkv_kernel,
        grid=(m // tm, N_HEADS // hpt),
        in_specs=[
            pl.BlockSpec((tm, d), lambda i, j: (i, 0)),
            _ada_spec(layer, 0, row_fn, d),
            _ada_spec(layer, 1, row_fn, d),
            pl.BlockSpec((None, d, tn), lambda i, j: (j_na, 0, j)),
        ],
        out_specs=[
            pl.BlockSpec((hpt, tm, HEAD_DIM), lambda i, j: (j, i, 0)),
            pl.BlockSpec((hpt, tm, HEAD_DIM), lambda i, j: (j, i, 0)),
            pl.BlockSpec((hpt, HEAD_DIM, tm), lambda i, j: (j, 0, i)),
        ],
        out_shape=[
            jax.ShapeDtypeStruct((N_HEADS, m, HEAD_DIM), BF16),
            jax.ShapeDtypeStruct((N_HEADS, m, HEAD_DIM), BF16),
            jax.ShapeDtypeStruct((N_HEADS, HEAD_DIM, m), BF16),
        ],
        scratch_shapes=[pltpu.VMEM((tm, d), BF16)],
        compiler_params=_cparams("parallel", "arbitrary"),
        name="qkv_proj",
    )(x, ada, ada, w_qkv)


def _bias_tiles(rpb):
    n_dr, n_dc = rpb.shape[1], rpb.shape[2]
    period = 2 * GRID_W + 1
    rev = jnp.pad(rpb[..., ::-1], ((0, 0), (0, 0), (0, period - n_dc)))
    skew = jnp.tile(rev, (1, 1, GRID_W))[..., :GRID_W * (period - 1)].reshape(-1, n_dr, GRID_W, period - 1)
    toep = skew[..., WIN_COLS - 1:WIN_COLS - 1 + GRID_W]
    first = jnp.concatenate([toep, toep[:, n_dr - 1:], toep[:, 3:4], toep[:, 11:12]], axis=1)
    second = jnp.concatenate([toep[:, :1], toep, toep[:, 2:3], toep[:, 10:11]], axis=1)
    vals = jnp.concatenate([first, second], axis=-1) * LOG2E

    e = np.arange(N_BIAS_TILES)[:, None, None]
    ck = np.arange(GRID_W)[None, :, None]
    lane = np.arange(2 * GRID_W)[None, None, :]
    half, cq = lane // GRID_W, lane % GRID_W
    d = np.where(e == 16, -4, np.where(e == 17, 4, e - (WIN_ROWS - 1)))
    dr = d - half
    row_ok = (np.abs(dr) <= WIN_ROWS - 1) & ((e != 16) | (half == 0)) & ((e != 17) | (half == 1))
    ws = np.clip(cq - WIN_COLS // 2, 0, GRID_W - WIN_COLS)
    ok = row_ok & (ck >= ws) & (ck < ws + WIN_COLS)
    return jnp.where(jnp.asarray(ok)[None], vals, -jnp.inf).astype(F32)


def _tile_entry(d):
    return d + WIN_ROWS - 1


_TILES_TOP = [[(j, _tile_entry(j - 2 * u)) for j in range(0, WIN_ROWS)] for u in range(Q_ROWS // 2)]
_TILES_BOT = [[(j, _tile_entry(j - WIN_ROWS - 2 * u)) for j in range(K_ROWS - WIN_ROWS, K_ROWS)]
              for u in range(Q_ROWS // 2)]
_TILES_MID = [[(j, 16 if j - 2 * u - 4 == -4 else 17 if j - 2 * u - 4 == 4 else _tile_entry(j - 2 * u - 4))
               for j in range(2 * u, 2 * u + WIN_ROWS + 1)] for u in range(Q_ROWS // 2)]


def _na_kernel(q_ref, k_ref, vt_ref, kc_ref, vct_ref, tab_ref, o_ref, st_ref, pt_ref, *, n_blocks, n_ctx):
    lanes = 2 * GRID_W
    n_ctx_tiles = n_ctx // GRID_W
    n_pairs = Q_ROWS // 2
    last = n_blocks - 1
    zero_tile = jnp.zeros((GRID_W, lanes), BF16)
    nt = (((1,), (1,)), ((), ()))

    def tok(b):
        return b * Q_TOK if isinstance(b, int) else pl.multiple_of(b * Q_TOK, Q_TOK)

    def key_block(b):
        hi = n_blocks - K_ROWS // Q_ROWS
        return min(max(b - 1, 0), hi) if isinstance(b, int) else jnp.clip(b - 1, 0, hi)

    def zero_unused(pslot, tiles):
        for u in range(n_pairs):
            used = dict(tiles[u])
            for j in range(K_ROWS):
                if j not in used:
                    pt_ref[pslot, j * GRID_W:(j + 1) * GRID_W, u * lanes:(u + 1) * lanes] = zero_tile

    def scores(b, sslot):
        q = q_ref[pl.ds(tok(b), Q_TOK), :]
        k0 = tok(key_block(b))
        st_ref[sslot, 0:K_TOK, :] = lax.dot_general(k_ref[pl.ds(k0, K_TOK), :], q, nt, preferred_element_type=F32)
        st_ref[sslot, K_TOK:, :] = lax.dot_general(kc_ref[...], q, nt, preferred_element_type=F32)

    def softmax(sslot, pslot, tiles):
        inv_l = []
        for u in range(n_pairs):
            cols = slice(u * lanes, (u + 1) * lanes)
            def tile_scores(i):
                if i < len(tiles[u]):
                    j, e = tiles[u][i]
                    return j, st_ref[sslot, j * GRID_W:(j + 1) * GRID_W, cols] + tab_ref[e]
                j = K_ROWS + i - len(tiles[u])
                return j, st_ref[sslot, j * GRID_W:(j + 1) * GRID_W, cols]

            n_tiles = len(tiles[u]) + n_ctx_tiles
            m = functools.reduce(jnp.maximum, [tile_scores(i)[1] for i in range(n_tiles)])
            m = jnp.max(m, axis=0, keepdims=True)
            l = None
            for i in range(n_tiles):
                j, s = tile_scores(i)
                p = jnp.exp2(s - m)
                l = p if l is None else l + p
                pt_ref[pslot, j * GRID_W:(j + 1) * GRID_W, cols] = p.astype(BF16)
            inv_l.append(1.0 / jnp.sum(l, axis=0, keepdims=True))
        return jnp.concatenate(inv_l, axis=1)

    def values(b, pslot, inv_l):
        k0 = tok(key_block(b))
        ot = jnp.dot(vt_ref[:, pl.ds(k0, K_TOK)], pt_ref[pslot, 0:K_TOK, :], preferred_element_type=F32)
        ot += jnp.dot(vct_ref[...], pt_ref[pslot, K_TOK:, :], preferred_element_type=F32)
        o_ref[pl.ds(tok(b), Q_TOK), :] = (ot * inv_l).T.astype(BF16)

    zero_unused(0, _TILES_MID)
    zero_unused(1, _TILES_MID)
    zero_unused(2, _TILES_TOP)
    zero_unused(3, _TILES_BOT)

    scores(0, 0)
    scores(1, 1)
    inv_l = softmax(0, 2, _TILES_TOP)
    scores(2, 0)
    inv_l_odd = softmax(1, 1, _TILES_MID)
    values(0, 2, inv_l)

    def body(t, inv_l_odd):
        b = 2 * t
        scores(b + 1, 1)
        inv_l_even = softmax(0, 0, _TILES_MID)
        values(b - 1, 1, inv_l_odd)
        scores(b + 2, 0)
        inv_l_odd = softmax(1, 1, _TILES_MID)
        values(b, 0, inv_l_even)
        return inv_l_odd

    inv_l_odd = lax.fori_loop(1, n_blocks // 2 - 1, body, inv_l_odd)
    scores(last, 1)
    inv_l_even = softmax(0, 0, _TILES_MID)
    values(last - 2, 1, inv_l_odd)
    inv_l_last = softmax(1, 3, _TILES_BOT)
    values(last - 1, 0, inv_l_even)
    values(last, 3, inv_l_last)


def _neighbourhood_attention(q, k, vt, k_c, vt_c, tab, batch):
    m = q.shape[1]
    seq = m // batch
    n_ctx = k_c.shape[1] // batch
    n_blocks = seq // Q_TOK
    assert n_blocks % 2 == 0 and n_blocks >= 4
    kern = functools.partial(_na_kernel, n_blocks=n_blocks, n_ctx=n_ctx)
    return pl.pallas_call(
        kern,
        grid=(batch, N_HEADS),
        in_specs=[
            pl.BlockSpec((None, seq, HEAD_DIM), lambda b, h: (h, b, 0)),
            pl.BlockSpec((None, seq, HEAD_DIM), lambda b, h: (h, b, 0)),
            pl.BlockSpec((None, HEAD_DIM, seq), lambda b, h: (h, 0, b)),
            pl.BlockSpec((None, n_ctx, HEAD_DIM), lambda b, h: (h, b, 0)),
            pl.BlockSpec((None, HEAD_DIM, n_ctx), lambda b, h: (h, 0, b)),
            pl.BlockSpec((None, N_BIAS_TILES, GRID_W, 2 * GRID_W), lambda b, h: (h, 0, 0, 0)),
        ],
        out_specs=pl.BlockSpec((None, seq, HEAD_DIM), lambda b, h: (h, b, 0)),
        out_shape=jax.ShapeDtypeStruct((N_HEADS, m, HEAD_DIM), BF16),
        scratch_shapes=[
            pltpu.VMEM((2, K_TOK + n_ctx, Q_TOK), F32),
            pltpu.VMEM((4, K_TOK + n_ctx, Q_TOK), BF16),
        ],
        compiler_params=_cparams("parallel", "parallel"),
        name="neighbourhood_attention",
    )(q, k, vt, k_c, vt_c, tab)


def _ctx_attn_kernel(q_ref, k_ref, vt_ref, o_ref):
    nt = (((1,), (1,)), ((), ()))
    st = lax.dot_general(k_ref[...], q_ref[...], nt, preferred_element_type=F32)
    p = jnp.exp2(st - jnp.max(st, axis=0, keepdims=True))
    inv_l = 1.0 / jnp.sum(p, axis=0, keepdims=True)
    ot = jnp.dot(vt_ref[...], p.astype(BF16), preferred_element_type=F32) * inv_l
    o_ref[...] = ot.T.astype(BF16)


def _context_attention(q_c, k_c, vt_c, batch):
    m = q_c.shape[1]
    n_ctx = m // batch
    return pl.pallas_call(
        _ctx_attn_kernel,
        grid=(batch, N_HEADS),
        in_specs=[
            pl.BlockSpec((None, n_ctx, HEAD_DIM), lambda b, h: (h, b, 0)),
            pl.BlockSpec((None, n_ctx, HEAD_DIM), lambda b, h: (h, b, 0)),
            pl.BlockSpec((None, HEAD_DIM, n_ctx), lambda b, h: (h, 0, b)),
        ],
        out_specs=pl.BlockSpec((None, n_ctx, HEAD_DIM), lambda b, h: (h, b, 0)),
        out_shape=jax.ShapeDtypeStruct((N_HEADS, m, HEAD_DIM), BF16),
        compiler_params=_cparams("parallel", "parallel"),
        name="context_attention",
    )(q_c, k_c, vt_c)


def _proj_ln_kernel(a_ref, w_ref, x_ref, g_ref, lng_ref, lnb_ref, o_ref, acat_ref, *, alpha):
    for h in range(N_HEADS):
        acat_ref[:, h * HEAD_DIM:(h + 1) * HEAD_DIM] = a_ref[h]
    y = jnp.dot(acat_ref[...], w_ref[...], preferred_element_type=F32)
    o_ref[...] = _layer_norm(alpha * x_ref[...] + g_ref[...] * y, lng_ref[...], lnb_ref[...])


def _proj_ln(a, w_o, j_na, x, ada, layer, row_fn, ln_g, ln_b, alpha, tm):
    m, d = x.shape
    return pl.pallas_call(
        functools.partial(_proj_ln_kernel, alpha=alpha),
        grid=(m // tm,),
        in_specs=[
            pl.BlockSpec((N_HEADS, tm, HEAD_DIM), lambda i: (0, i, 0)),
            pl.BlockSpec((None, d, d), lambda i: (j_na, 0, 0)),
            pl.BlockSpec((tm, d), lambda i: (i, 0)),
            _ada_spec(layer, 2, row_fn, d),
            _vec_spec(layer, d),
            _vec_spec(layer, d),
        ],
        out_specs=pl.BlockSpec((tm, d), lambda i: (i, 0)),
        out_shape=jax.ShapeDtypeStruct((m, d), F32),
        scratch_shapes=[pltpu.VMEM((tm, d), BF16)],
        compiler_params=_cparams("parallel"),
        name="out_proj_norm",
    )(a, w_o, x, ada, ln_g, ln_b)


def _pool_kernel(x_ref, xp_ref, xn_ref, sh_ref, sc_ref, g_ref, w_ref, ps_ref, lng_ref, lnb_ref, o_ref, ext_ref,
                 *, alpha, seq_len):
    tm, d = x_ref.shape
    gw = d // len(POOL_WINDOWS)
    start = pl.program_id(0) * tm
    pos0 = start % seq_len
    sc1 = 1.0 + sc_ref[...]
    sh = sh_ref[...]
    h = x_ref[...] * sc1 + sh
    ext_ref[0:POOL_HALO, :] = jnp.where(pos0 > 0, xp_ref[...] * sc1 + sh, 0.0)
    ext_ref[POOL_HALO:POOL_HALO + tm, :] = h
    ext_ref[POOL_HALO + tm:, :] = jnp.where(pos0 + tm < seq_len, xn_ref[...] * sc1 + sh, 0.0)
    pos = pos0 + lax.broadcasted_iota(jnp.int32, (tm, gw), 0)
    ys = []
    for gi, w in enumerate(POOL_WINDOWS):
        cols = slice(gi * gw, (gi + 1) * gw)
        tot = functools.reduce(
            jnp.add, [ext_ref[POOL_HALO + o:POOL_HALO + o + tm, cols] for o in range(-(w // 2), w - w // 2)])
        cnt = jnp.minimum(pos - w // 2 + w, seq_len) - jnp.maximum(pos - w // 2, 0)
        pooled = tot / cnt.astype(F32) - h[:, cols]
        ys.append(jnp.dot(pooled.astype(BF16), w_ref[gi], preferred_element_type=F32))
    y = jnp.concatenate(ys, axis=1) * ps_ref[...]
    o_ref[...] = _layer_norm(alpha * x_ref[...] + g_ref[...] * y, lng_ref[...], lnb_ref[...])


def _pool_mix(x, ada, layer, row_fn, w_pool, pool_scale, j_pool, ln_g, ln_b, alpha, tm, seq_len):
    m, d = x.shape
    hb = tm // POOL_HALO
    n_halo_blocks = m // POOL_HALO
    gw = d // len(POOL_WINDOWS)
    return pl.pallas_call(
        functools.partial(_pool_kernel, alpha=alpha, seq_len=seq_len),
        grid=(m // tm,),
        in_specs=[
            pl.BlockSpec((tm, d), lambda i: (i, 0)),
            pl.BlockSpec((POOL_HALO, d), lambda i: (jnp.maximum(i * hb - 1, 0), 0)),
            pl.BlockSpec((POOL_HALO, d), lambda i: (jnp.minimum((i + 1) * hb, n_halo_blocks - 1), 0)),
            _ada_spec(layer, 0, row_fn, d),
            _ada_spec(layer, 1, row_fn, d),
            _ada_spec(layer, 2, row_fn, d),
            pl.BlockSpec((None, len(POOL_WINDOWS), gw, gw), lambda i: (j_pool, 0, 0, 0)),
            _vec_spec(j_pool, d),
            _vec_spec(layer, d),
            _vec_spec(layer, d),
        ],
        out_specs=pl.BlockSpec((tm, d), lambda i: (i, 0)),
        out_shape=jax.ShapeDtypeStruct((m, d), F32),
        scratch_shapes=[pltpu.VMEM((tm + 2 * POOL_HALO, d), F32)],
        compiler_params=_cparams("parallel"),
        name="pool_mix_norm",
    )(x, x, x, ada, ada, ada, w_pool, pool_scale, ln_g, ln_b)


def _ffn_kernel(x_ref, sh_ref, sc_ref, g_ref, wg_ref, wu_ref, wo_ref, lng_ref, lnb_ref, o_ref, h_ref, *, alpha):
    f = pl.program_id(1)

    @pl.when(f == 0)
    def _():
        _modulate_rows(x_ref, sh_ref, sc_ref, h_ref)
        o_ref[...] = jnp.zeros_like(o_ref)

    h = h_ref[...]
    gate = jnp.dot(h, wg_ref[...], preferred_element_type=F32)
    up = jnp.dot(h, wu_ref[...], preferred_element_type=F32)
    o_ref[...] += jnp.dot((_silu(gate) * up).astype(BF16), wo_ref[...], preferred_element_type=F32)

    @pl.when(f == pl.num_programs(1) - 1)
    def _():
        g, lng, lnb = g_ref[...], lng_ref[...], lnb_ref[...]

        def chunk(rows):
            o_ref[rows, :] = _layer_norm(alpha * x_ref[rows, :] + g * o_ref[rows, :], lng, lnb)
        _for_row_chunks(o_ref.shape[0], chunk)


def _ffn(x, ada, layer, row_fn, w_in, w_out, ln_g, ln_b, alpha, tm, tf=512):
    m, d = x.shape
    d_ff = w_out.shape[1]
    nf = d_ff // tf
    return pl.pallas_call(
        functools.partial(_ffn_kernel, alpha=alpha),
        grid=(m // tm, nf),
        in_specs=[
            pl.BlockSpec((tm, d), lambda i, f: (i, 0)),
            _ada_spec(layer, 3, row_fn, d),
            _ada_spec(layer, 4, row_fn, d),
            _ada_spec(layer, 5, row_fn, d),
            pl.BlockSpec((None, d, tf), lambda i, f: (layer, 0, f)),
            pl.BlockSpec((None, d, tf), lambda i, f: (layer, 0, nf + f)),
            pl.BlockSpec((None, tf, d), lambda i, f: (layer, f, 0)),
            _vec_spec(layer, d),
            _vec_spec(layer, d),
        ],
        out_specs=pl.BlockSpec((tm, d), lambda i, f: (i, 0)),
        out_shape=jax.ShapeDtypeStruct((m, d), F32),
        scratch_shapes=[pltpu.VMEM((tm, d), BF16)],
        compiler_params=_cparams("parallel", "arbitrary"),
        name="ffn_norm",
    )(x, ada, ada, ada, w_in, w_in, w_out, ln_g, ln_b)


def kernel(x, c, ctx, c_ctx, ada_w, ada_b, ln_mix_g, ln_mix_b, ln_ffn_g, ln_ffn_b,
           na_w_qkv, na_w_o, na_rpb, pool_w, pool_scale, ffn_w_in, ffn_w_out):
    batch, seq, d = x.shape
    n_ctx = ctx.shape[1]
    depth = ada_w.shape[0]
    n_mixers = 2
    assert seq % Q_TOK == 0 and seq == GRID_W * GRID_W and batch < COND_ROWS
    alpha = (2 * depth) ** 0.25
    last_na = max(i for i in range(depth) if i % n_mixers == 0)

    cond = jnp.concatenate([c, c_ctx[None], jnp.zeros((COND_ROWS - batch - 1, d), F32)], axis=0)
    ada = _ada_params(cond, ada_w, ada_b).reshape(depth, COND_ROWS, N_ADA, 1, d)

    tm = 512
    tm_big = 1024
    lat_row = lambda i: i // (seq // tm)
    lat_row_big = lambda i: i // (seq // tm_big)
    ctx_row = lambda i: batch
    ctx_tm = min(tm, batch * n_ctx)

    qscale = jnp.concatenate([jnp.full((d,), HEAD_DIM ** -0.5 * LOG2E, F32), jnp.ones((2 * d,), F32)])
    n_groups = N_HEADS // QKV_HEADS_PER_TILE
    w_qkv = (na_w_qkv * qscale).astype(BF16).reshape(-1, d, 3, n_groups, QKV_HEADS_PER_TILE * HEAD_DIM)
    w_qkv = w_qkv.transpose(0, 1, 3, 2, 4).reshape(-1, d, 3 * d)
    w_o = na_w_o.astype(BF16)
    w_pool = pool_w.astype(BF16)
    w_in = ffn_w_in.astype(BF16)
    w_out = ffn_w_out.astype(BF16)
    lmg, lmb = ln_mix_g.reshape(depth, 1, d), ln_mix_b.reshape(depth, 1, d)
    lfg, lfb = ln_ffn_g.reshape(depth, 1, d), ln_ffn_b.reshape(depth, 1, d)
    ps = pool_scale.reshape(-1, 1, d)

    xl = x.reshape(batch * seq, d)
    xc = ctx.reshape(batch * n_ctx, d)
    for i in range(depth):
        use_na = i % n_mixers == 0
        j = i // n_mixers
        update_ctx = i < last_na
        if use_na:
            q, k, vt = _qkv_proj(xl, ada, i, lat_row_big, w_qkv, j, tm_big)
            q_c, k_c, vt_c = _qkv_proj(xc, ada, i, ctx_row, w_qkv, j, batch * n_ctx)
            att = _neighbourhood_attention(q, k, vt, k_c, vt_c, _bias_tiles(na_rpb[j]), batch)
            xl = _proj_ln(att, w_o, j, xl, ada, i, lat_row, lmg, lmb, alpha, tm)
            if update_ctx:
                att_c = _context_attention(q_c, k_c, vt_c, batch)
                xc = _proj_ln(att_c, w_o, j, xc, ada, i, ctx_row, lmg, lmb, alpha, ctx_tm)
        else:
            xl = _pool_mix(xl, ada, i, lat_row, w_pool, ps, j, lmg, lmb, alpha, tm, seq)
            if update_ctx:
                xc = _pool_mix(xc, ada, i, ctx_row, w_pool, ps, j, lmg, lmb, alpha, n_ctx, n_ctx)
        xl = _ffn(xl, ada, i, lat_row_big, w_in, w_out, lfg, lfb, alpha, tm_big)
        if update_ctx:
            xc = _ffn(xc, ada, i, ctx_row, w_in, w_out, lfg, lfb, alpha, min(tm_big, batch * n_ctx))
    return xl.reshape(batch, seq, d)
```

```python
import functools
import math

import numpy as np
import jax
import jax.numpy as jnp
from jax import lax
from jax.experimental import pallas as pl
from jax.experimental.pallas import tpu as pltpu

F32 = jnp.float32
BF16 = jnp.bfloat16

GRID_W = 64
N_HEADS = 16
HEAD_DIM = 128
WIN_ROWS = 8
WIN_COLS = 16
POOL_WINDOWS = (2, 4, 8, 16)
N_ADA = 6
LN_EPS = 1e-5
LOG2E = math.log2(math.e)

Q_ROWS = 4
K_ROWS = Q_ROWS + WIN_ROWS
Q_TOK = Q_ROWS * GRID_W
K_TOK = K_ROWS * GRID_W
N_BIAS_TILES = 18
NA_OUT_CHUNK = 1024
POOL_HALO = 8
COND_ROWS = 8
ROW_CHUNK = 128
QKV_HEADS_PER_TILE = 4

VMEM_LIMIT = 58 * 1024 * 1024


def _cparams(*sem):
    return pltpu.CompilerParams(dimension_semantics=sem, vmem_limit_bytes=VMEM_LIMIT)


def _layer_norm(v, g, b):
    mu = jnp.mean(v, axis=-1, keepdims=True)
    d = v - mu
    var = jnp.mean(d * d, axis=-1, keepdims=True)
    return d * lax.rsqrt(var + LN_EPS) * g + b


def _silu(v):
    return v * jax.nn.sigmoid(v)


def _for_row_chunks(n_rows, fn):
    def body(c, carry):
        fn(pl.ds(pl.multiple_of(c * ROW_CHUNK, ROW_CHUNK), ROW_CHUNK))
        return carry
    lax.fori_loop(0, n_rows // ROW_CHUNK, body, 0)


def _modulate_rows(x_ref, sh_ref, sc_ref, h_ref):
    sc1 = 1.0 + sc_ref[...]
    sh = sh_ref[...]

    def chunk(rows):
        h_ref[rows, :] = (x_ref[rows, :] * sc1 + sh).astype(BF16)
    _for_row_chunks(x_ref.shape[0], chunk)


def _ada_kernel(cond_ref, w_ref, b_ref, o_ref):
    s = _silu(cond_ref[...]).astype(BF16)
    o_ref[...] = jnp.dot(s, w_ref[...].astype(BF16), preferred_element_type=F32) + b_ref[...]


def _ada_params(cond, ada_w, ada_b, tn=1024):
    depth, d, n = ada_w.shape
    return pl.pallas_call(
        _ada_kernel,
        grid=(depth, n // tn),
        in_specs=[
            pl.BlockSpec((COND_ROWS, d), lambda l, j: (0, 0)),
            pl.BlockSpec((None, d, tn), lambda l, j: (l, 0, j)),
            pl.BlockSpec((None, 1, tn), lambda l, j: (l, 0, j)),
        ],
        out_specs=pl.BlockSpec((None, COND_ROWS, tn), lambda l, j: (l, 0, j)),
        out_shape=jax.ShapeDtypeStruct((depth, COND_ROWS, n), F32),
        compiler_params=_cparams("parallel", "parallel"),
        name="ada_params",
    )(cond, ada_w, ada_b.reshape(depth, 1, n))


def _ada_spec(layer, which, row_fn, d):
    return pl.BlockSpec((None, None, None, 1, d), lambda i, *_: (layer, row_fn(i), which, 0, 0))


def _vec_spec(layer, d):
    return pl.BlockSpec((None, 1, d), lambda i, *_: (layer, 0, 0))


def _qkv_kernel(x_ref, sh_ref, sc_ref, wq_ref, wk_ref, wv_ref, q_ref, k_ref, vt_ref, h_ref):
    @pl.when(pl.program_id(1) == 0)
    def _():
        _modulate_rows(x_ref, sh_ref, sc_ref, h_ref)

    h = h_ref[...]
    for w_ref, o_ref, transposed in ((wq_ref, q_ref, False), (wk_ref, k_ref, False), (wv_ref, vt_ref, True)):
        res = jnp.dot(h, w_ref[...], preferred_element_type=F32)
        for hh in range(o_ref.shape[0]):
            part = res[:, hh * HEAD_DIM:(hh + 1) * HEAD_DIM]
            o_ref[hh] = (part.T if transposed else part).astype(BF16)


def _qkv_proj(x, ada, layer, row_fn, w_qkv, j_na, tm):
    m, d = x.shape
    hpt = QKV_HEADS_PER_TILE
    tn = hpt * HEAD_DIM
    n_groups = N_HEADS // hpt
    return pl.pallas_call(
        _qkv_kernel,
        grid=(m // tm, n_groups),
        in_specs=[
            pl.BlockSpec((tm, d), lambda i, j: (i, 0)),
            _ada_spec(layer, 0, row_fn, d),
            _ada_spec(layer, 1, row_fn, d),
            pl.BlockSpec((None, d, tn), lambda i, j: (j_na, 0, j)),
            pl.BlockSpec((None, d, tn), lambda i, j: (j_na, 0, n_groups + j)),
            pl.BlockSpec((None, d, tn), lambda i, j: (j_na, 0, 2 * n_groups + j)),
        ],
        out_specs=[
            pl.BlockSpec((hpt, tm, HEAD_DIM), lambda i, j: (j, i, 0)),
            pl.BlockSpec((hpt, tm, HEAD_DIM), lambda i, j: (j, i, 0)),
            pl.BlockSpec((hpt, HEAD_DIM, tm), lambda i, j: (j, 0, i)),
        ],
        out_shape=[
            jax.ShapeDtypeStruct((N_HEADS, m, HEAD_DIM), BF16),
            jax.ShapeDtypeStruct((N_HEADS, m, HEAD_DIM), BF16),
            jax.ShapeDtypeStruct((N_HEADS, HEAD_DIM, m), BF16),
        ],
        scratch_shapes=[pltpu.VMEM((tm, d), BF16)],
        compiler_params=_cparams("parallel", "arbitrary"),
        name="qkv_proj",
    )(x, ada, ada, w_qkv, w_qkv, w_qkv)


def _bias_tiles(rpb):
    n_dr, n_dc = rpb.shape[1], rpb.shape[2]
    period = 2 * GRID_W + 1
    rev = jnp.pad(rpb[..., ::-1], ((0, 0), (0, 0), (0, period - n_dc)))
    skew = jnp.tile(rev, (1, 1, GRID_W))[..., :GRID_W * (period - 1)].reshape(-1, n_dr, GRID_W, period - 1)
    toep = skew[..., WIN_COLS - 1:WIN_COLS - 1 + GRID_W]
    first = jnp.concatenate([toep, toep[:, n_dr - 1:], toep[:, 3:4], toep[:, 11:12]], axis=1)
    second = jnp.concatenate([toep[:, :1], toep, toep[:, 2:3], toep[:, 10:11]], axis=1)
    vals = jnp.concatenate([first, second], axis=-1) * LOG2E

    e = np.arange(N_BIAS_TILES)[:, None, None]
    ck = np.arange(GRID_W)[None, :, None]
    lane = np.arange(2 * GRID_W)[None, None, :]
    half, cq = lane // GRID_W, lane % GRID_W
    d = np.where(e == 16, -4, np.where(e == 17, 4, e - (WIN_ROWS - 1)))
    dr = d - half
    row_ok = (np.abs(dr) <= WIN_ROWS - 1) & ((e != 16) | (half == 0)) & ((e != 17) | (half == 1))
    ws = np.clip(cq - WIN_COLS // 2, 0, GRID_W - WIN_COLS)
    ok = row_ok & (ck >= ws) & (ck < ws + WIN_COLS)
    return jnp.where(jnp.asarray(ok)[None], vals, -jnp.inf).astype(F32)


def _tile_entry(d):
    return d + WIN_ROWS - 1


_TILES_TOP = [[(j, _tile_entry(j - 2 * u)) for j in range(0, WIN_ROWS)] for u in range(Q_ROWS // 2)]
_TILES_BOT = [[(j, _tile_entry(j - WIN_ROWS - 2 * u)) for j in range(K_ROWS - WIN_ROWS, K_ROWS)]
              for u in range(Q_ROWS // 2)]
_TILES_MID = [[(j, 16 if j - 2 * u - 4 == -4 else 17 if j - 2 * u - 4 == 4 else _tile_entry(j - 2 * u - 4))
               for j in range(2 * u, 2 * u + WIN_ROWS + 1)] for u in range(Q_ROWS // 2)]


def _na_kernel(q_ref, k_ref, vt_ref, kc_ref, vct_ref, tab_ref, o_ref,
               st_ref, pt_ref, sct_ref, pct_ref, otw_ref, linv_ref, *, n_blocks, n_ctx):
    lanes = 2 * GRID_W
    n_ctx_tiles = n_ctx // GRID_W
    n_pairs = Q_ROWS // 2
    last = n_blocks - 1
    zero_tile = jnp.zeros((GRID_W, lanes), BF16)
    nt = (((1,), (1,)), ((), ()))

    def tok(b):
        return b * Q_TOK if isinstance(b, int) else pl.multiple_of(b * Q_TOK, Q_TOK)

    def key_block(b):
        hi = n_blocks - K_ROWS // Q_ROWS
        return min(max(b - 1, 0), hi) if isinstance(b, int) else jnp.clip(b - 1, 0, hi)

    def zero_unused(pslot, tiles):
        for u in range(n_pairs):
            used = dict(tiles[u])
            for j in range(K_ROWS):
                if j not in used:
                    pt_ref[pslot, j * GRID_W:(j + 1) * GRID_W, u * lanes:(u + 1) * lanes] = zero_tile

    def scores(b, sslot):
        q = q_ref[pl.ds(tok(b), Q_TOK), :]
        st_ref[sslot] = lax.dot_general(k_ref[pl.ds(tok(key_block(b)), K_TOK), :], q, nt,
                                        preferred_element_type=F32)

    def softmax(b, sslot, pslot, tiles):
        for u in range(n_pairs):
            cols = slice(u * lanes, (u + 1) * lanes)
            qcols = pl.ds(tok(b) + u * lanes, lanes)

            def tile_scores(i):
                if i < len(tiles[u]):
                    j, e = tiles[u][i]
                    return st_ref[sslot, j * GRID_W:(j + 1) * GRID_W, cols] + tab_ref[e]
                t = i - len(tiles[u])
                return sct_ref[t * GRID_W:(t + 1) * GRID_W, qcols]

            n_tiles = len(tiles[u]) + n_ctx_tiles
            m = functools.reduce(jnp.maximum, [tile_scores(i) for i in range(n_tiles)])
            m = jnp.max(m, axis=0, keepdims=True)
            l = None
            for i in range(n_tiles):
                p = jnp.exp2(tile_scores(i) - m)
                l = p if l is None else l + p
                if i < len(tiles[u]):
                    j = tiles[u][i][0]
                    pt_ref[pslot, j * GRID_W:(j + 1) * GRID_W, cols] = p.astype(BF16)
                else:
                    t = i - len(tiles[u])
                    pct_ref[t * GRID_W:(t + 1) * GRID_W, qcols] = p.astype(BF16)
            linv_ref[0:1, qcols] = 1.0 / jnp.sum(l, axis=0, keepdims=True)

    def values(b, pslot):
        otw_ref[:, pl.ds(tok(b), Q_TOK)] = jnp.dot(vt_ref[:, pl.ds(tok(key_block(b)), K_TOK)], pt_ref[pslot],
                                                   preferred_element_type=F32)

    zero_unused(0, _TILES_MID)
    zero_unused(1, _TILES_MID)
    zero_unused(2, _TILES_TOP)
    zero_unused(3, _TILES_BOT)
    sct_ref[...] = lax.dot_general(kc_ref[...], q_ref[...], nt, preferred_element_type=F32)

    scores(0, 0)
    scores(1, 1)
    softmax(0, 0, 2, _TILES_TOP)
    scores(2, 0)
    softmax(1, 1, 1, _TILES_MID)
    values(0, 2)

    def body(t, carry):
        b = 2 * t
        scores(b + 1, 1)
        softmax(b, 0, 0, _TILES_MID)
        values(b - 1, 1)
        scores(b + 2, 0)
        softmax(b + 1, 1, 1, _TILES_MID)
        values(b, 0)
        return carry

    lax.fori_loop(1, n_blocks // 2 - 1, body, 0)
    scores(last, 1)
    softmax(last - 1, 0, 0, _TILES_MID)
    values(last - 2, 1)
    softmax(last, 1, 3, _TILES_BOT)
    values(last - 1, 0)
    values(last, 3)

    for c in range(n_blocks * Q_TOK // NA_OUT_CHUNK):
        cols = slice(c * NA_OUT_CHUNK, (c + 1) * NA_OUT_CHUNK)
        ot = otw_ref[:, cols] + jnp.dot(vct_ref[...], pct_ref[:, cols], preferred_element_type=F32)
        o_ref[cols, :] = (ot * linv_ref[0:1, cols]).T.astype(BF16)


def _neighbourhood_attention(q, k, vt, k_c, vt_c, tab, batch):
    m = q.shape[1]
    seq = m // batch
    n_ctx = k_c.shape[1] // batch
    n_blocks = seq // Q_TOK
    assert n_blocks % 2 == 0 and n_blocks >= 4 and seq % NA_OUT_CHUNK == 0
    kern = functools.partial(_na_kernel, n_blocks=n_blocks, n_ctx=n_ctx)
    return pl.pallas_call(
        kern,
        grid=(batch, N_HEADS),
        in_specs=[
            pl.BlockSpec((None, seq, HEAD_DIM), lambda b, h: (h, b, 0)),
            pl.BlockSpec((None, seq, HEAD_DIM), lambda b, h: (h, b, 0)),
            pl.BlockSpec((None, HEAD_DIM, seq), lambda b, h: (h, 0, b)),
            pl.BlockSpec((None, n_ctx, HEAD_DIM), lambda b, h: (h, b, 0)),
            pl.BlockSpec((None, HEAD_DIM, n_ctx), lambda b, h: (h, 0, b)),
            pl.BlockSpec((None, N_BIAS_TILES, GRID_W, 2 * GRID_W), lambda b, h: (h, 0, 0, 0)),
        ],
        out_specs=pl.BlockSpec((None, seq, HEAD_DIM), lambda b, h: (h, b, 0)),
        out_shape=jax.ShapeDtypeStruct((N_HEADS, m, HEAD_DIM), BF16),
        scratch_shapes=[
            pltpu.VMEM((2, K_TOK, Q_TOK), F32),
            pltpu.VMEM((4, K_TOK, Q_TOK), BF16),
            pltpu.VMEM((n_ctx, seq), F32),
            pltpu.VMEM((n_ctx, seq), BF16),
            pltpu.VMEM((HEAD_DIM, seq), F32),
            pltpu.VMEM((8, seq), F32),
        ],
        compiler_params=_cparams("parallel", "parallel"),
        name="neighbourhood_attention",
    )(q, k, vt, k_c, vt_c, tab)


def _ctx_attn_kernel(q_ref, k_ref, vt_ref, o_ref):
    nt = (((1,), (1,)), ((), ()))
    st = lax.dot_general(k_ref[...], q_ref[...], nt, preferred_element_type=F32)
    p = jnp.exp2(st - jnp.max(st, axis=0, keepdims=True))
    inv_l = 1.0 / jnp.sum(p, axis=0, keepdims=True)
    ot = jnp.dot(vt_ref[...], p.astype(BF16), preferred_element_type=F32) * inv_l
    o_ref[...] = ot.T.astype(BF16)


def _context_attention(q_c, k_c, vt_c, batch):
    m = q_c.shape[1]
    n_ctx = m // batch
    return pl.pallas_call(
        _ctx_attn_kernel,
        grid=(batch, N_HEADS),
        in_specs=[
            pl.BlockSpec((None, n_ctx, HEAD_DIM), lambda b, h: (h, b, 0)),
            pl.BlockSpec((None, n_ctx, HEAD_DIM), lambda b, h: (h, b, 0)),
            pl.BlockSpec((None, HEAD_DIM, n_ctx), lambda b, h: (h, 0, b)),
        ],
        out_specs=pl.BlockSpec((None, n_ctx, HEAD_DIM), lambda b, h: (h, b, 0)),
        out_shape=jax.ShapeDtypeStruct((N_HEADS, m, HEAD_DIM), BF16),
        compiler_params=_cparams("parallel", "parallel"),
        name="context_attention",
    )(q_c, k_c, vt_c)


def _proj_ln_kernel(a_ref, w_ref, x_ref, g_ref, lng_ref, lnb_ref, o_ref, acat_ref, *, alpha):
    for h in range(N_HEADS):
        acat_ref[:, h * HEAD_DIM:(h + 1) * HEAD_DIM] = a_ref[h]
    y = jnp.dot(acat_ref[...], w_ref[...], preferred_element_type=F32)
    o_ref[...] = _layer_norm(alpha * x_ref[...] + g_ref[...] * y, lng_ref[...], lnb_ref[...])


def _proj_ln(a, w_o, j_na, x, ada, layer, row_fn, ln_g, ln_b, alpha, tm):
    m, d = x.shape
    return pl.pallas_call(
        functools.partial(_proj_ln_kernel, alpha=alpha),
        grid=(m // tm,),
        in_specs=[
            pl.BlockSpec((N_HEADS, tm, HEAD_DIM), lambda i: (0, i, 0)),
            pl.BlockSpec((None, d, d), lambda i: (j_na, 0, 0)),
            pl.BlockSpec((tm, d), lambda i: (i, 0)),
            _ada_spec(layer, 2, row_fn, d),
            _vec_spec(layer, d),
            _vec_spec(layer, d),
        ],
        out_specs=pl.BlockSpec((tm, d), lambda i: (i, 0)),
        out_shape=jax.ShapeDtypeStruct((m, d), F32),
        scratch_shapes=[pltpu.VMEM((tm, d), BF16)],
        compiler_params=_cparams("parallel"),
        name="out_proj_norm",
    )(a, w_o, x, ada, ln_g, ln_b)


def _pool_kernel(x_ref, xp_ref, xn_ref, sh_ref, sc_ref, g_ref, w_ref, ps_ref, lng_ref, lnb_ref, o_ref, ext_ref,
                 *, alpha, seq_len):
    tm, d = x_ref.shape
    gw = d // len(POOL_WINDOWS)
    start = pl.program_id(0) * tm
    pos0 = start % seq_len
    sc1 = 1.0 + sc_ref[...]
    sh = sh_ref[...]
    h = x_ref[...] * sc1 + sh
    ext_ref[0:POOL_HALO, :] = jnp.where(pos0 > 0, xp_ref[...] * sc1 + sh, 0.0)
    ext_ref[POOL_HALO:POOL_HALO + tm, :] = h
    ext_ref[POOL_HALO + tm:, :] = jnp.where(pos0 + tm < seq_len, xn_ref[...] * sc1 + sh, 0.0)
    pos = pos0 + lax.broadcasted_iota(jnp.int32, (tm, gw), 0)
    ys = []
    for gi, w in enumerate(POOL_WINDOWS):
        cols = slice(gi * gw, (gi + 1) * gw)
        tot = functools.reduce(
            jnp.add, [ext_ref[POOL_HALO + o:POOL_HALO + o + tm, cols] for o in range(-(w // 2), w - w // 2)])
        cnt = jnp.minimum(pos - w // 2 + w, seq_len) - jnp.maximum(pos - w // 2, 0)
        pooled = tot / cnt.astype(F32) - h[:, cols]
        ys.append(jnp.dot(pooled.astype(BF16), w_ref[gi], preferred_element_type=F32))
    y = jnp.concatenate(ys, axis=1) * ps_ref[...]
    o_ref[...] = _layer_norm(alpha * x_ref[...] + g_ref[...] * y, lng_ref[...], lnb_ref[...])


def _pool_mix(x, ada, layer, row_fn, w_pool, pool_scale, j_pool, ln_g, ln_b, alpha, tm, seq_len):
    m, d = x.shape
    hb = tm // POOL_HALO
    n_halo_blocks = m // POOL_HALO
    gw = d // len(POOL_WINDOWS)
    return pl.pallas_call(
        functools.partial(_pool_kernel, alpha=alpha, seq_len=seq_len),
        grid=(m // tm,),
        in_specs=[
            pl.BlockSpec((tm, d), lambda i: (i, 0)),
            pl.BlockSpec((POOL_HALO, d), lambda i: (jnp.maximum(i * hb - 1, 0), 0)),
            pl.BlockSpec((POOL_HALO, d), lambda i: (jnp.minimum((i + 1) * hb, n_halo_blocks - 1), 0)),
            _ada_spec(layer, 0, row_fn, d),
            _ada_spec(layer, 1, row_fn, d),
            _ada_spec(layer, 2, row_fn, d),
            pl.BlockSpec((None, len(POOL_WINDOWS), gw, gw), lambda i: (j_pool, 0, 0, 0)),
            _vec_spec(j_pool, d),
            _vec_spec(layer, d),
            _vec_spec(layer, d),
        ],
        out_specs=pl.BlockSpec((tm, d), lambda i: (i, 0)),
        out_shape=jax.ShapeDtypeStruct((m, d), F32),
        scratch_shapes=[pltpu.VMEM((tm + 2 * POOL_HALO, d), F32)],
        compiler_params=_cparams("parallel"),
        name="pool_mix_norm",
    )(x, x, x, ada, ada, ada, w_pool, pool_scale, ln_g, ln_b)


def _ffn_kernel(x_ref, sh_ref, sc_ref, g_ref, wg_ref, wu_ref, wo_ref, lng_ref, lnb_ref, o_ref, h_ref, *, alpha):
    f = pl.program_id(1)

    @pl.when(f == 0)
    def _():
        _modulate_rows(x_ref, sh_ref, sc_ref, h_ref)
        o_ref[...] = jnp.zeros_like(o_ref)

    h = h_ref[...]
    gate = jnp.dot(h, wg_ref[...], preferred_element_type=F32)
    up = jnp.dot(h, wu_ref[...], preferred_element_type=F32)
    o_ref[...] += jnp.dot((_silu(gate) * up).astype(BF16), wo_ref[...], preferred_element_type=F32)

    @pl.when(f == pl.num_programs(1) - 1)
    def _():
        g, lng, lnb = g_ref[...], lng_ref[...], lnb_ref[...]

        def chunk(rows):
            o_ref[rows, :] = _layer_norm(alpha * x_ref[rows, :] + g * o_ref[rows, :], lng, lnb)
        _for_row_chunks(o_ref.shape[0], chunk)


def _ffn(x, ada, layer, row_fn, w_in, w_out, ln_g, ln_b, alpha, tm, tf=512):
    m, d = x.shape
    d_ff = w_out.shape[1]
    nf = d_ff // tf
    return pl.pallas_call(
        functools.partial(_ffn_kernel, alpha=alpha),
        grid=(m // tm, nf),
        in_specs=[
            pl.BlockSpec((tm, d), lambda i, f: (i, 0)),
            _ada_spec(layer, 3, row_fn, d),
            _ada_spec(layer, 4, row_fn, d),
            _ada_spec(layer, 5, row_fn, d),
            pl.BlockSpec((None, d, tf), lambda i, f: (layer, 0, f)),
            pl.BlockSpec((None, d, tf), lambda i, f: (layer, 0, nf + f)),
            pl.BlockSpec((None, tf, d), lambda i, f: (layer, f, 0)),
            _vec_spec(layer, d),
            _vec_spec(layer, d),
        ],
        out_specs=pl.BlockSpec((tm, d), lambda i, f: (i, 0)),
        out_shape=jax.ShapeDtypeStruct((m, d), F32),
        scratch_shapes=[pltpu.VMEM((tm, d), BF16)],
        compiler_params=_cparams("parallel", "arbitrary"),
        name="ffn_norm",
    )(x, ada, ada, ada, w_in, w_in, w_out, ln_g, ln_b)


def kernel(x, c, ctx, c_ctx, ada_w, ada_b, ln_mix_g, ln_mix_b, ln_ffn_g, ln_ffn_b,
           na_w_qkv, na_w_o, na_rpb, pool_w, pool_scale, ffn_w_in, ffn_w_out):
    batch, seq, d = x.shape
    n_ctx = ctx.shape[1]
    depth = ada_w.shape[0]
    n_mixers = 2
    assert seq % Q_TOK == 0 and seq == GRID_W * GRID_W and batch < COND_ROWS
    alpha = (2 * depth) ** 0.25
    last_na = max(i for i in range(depth) if i % n_mixers == 0)

    cond = jnp.concatenate([c, c_ctx[None], jnp.zeros((COND_ROWS - batch - 1, d), F32)], axis=0)
    ada = _ada_params(cond, ada_w, ada_b).reshape(depth, COND_ROWS, N_ADA, 1, d)

    tm = 512
    tm_big = 1024
    lat_row = lambda i: i // (seq // tm)
    lat_row_big = lambda i: i // (seq // tm_big)
    ctx_row = lambda i: batch
    ctx_tm = min(tm, batch * n_ctx)

    qscale = jnp.concatenate([jnp.full((d,), HEAD_DIM ** -0.5 * LOG2E, F32), jnp.ones((2 * d,), F32)])
    w_qkv = (na_w_qkv * qscale).astype(BF16)
    w_o = na_w_o.astype(BF16)
    w_pool = pool_w.astype(BF16)
    w_in = ffn_w_in.astype(BF16)
    w_out = ffn_w_out.astype(BF16)
    lmg, lmb = ln_mix_g.reshape(depth, 1, d), ln_mix_b.reshape(depth, 1, d)
    lfg, lfb = ln_ffn_g.reshape(depth, 1, d), ln_ffn_b.reshape(depth, 1, d)
    ps = pool_scale.reshape(-1, 1, d)

    xl = x.reshape(batch * seq, d)
    xc = ctx.reshape(batch * n_ctx, d)
    for i in range(depth):
        use_na = i % n_mixers == 0
        j = i // n_mixers
        update_ctx = i < last_na
        if use_na:
            q, k, vt = _qkv_proj(xl, ada, i, lat_row_big, w_qkv, j, tm_big)
            q_c, k_c, vt_c = _qkv_proj(xc, ada, i, ctx_row, w_qkv, j, batch * n_ctx)
            att = _neighbourhood_attention(q, k, vt, k_c, vt_c, _bias_tiles(na_rpb[j]), batch)
            xl = _proj_ln(att, w_o, j, xl, ada, i, lat_row, lmg, lmb, alpha, tm)
            if update_ctx:
                att_c = _context_attention(q_c, k_c, vt_c, batch)
                xc = _proj_ln(att_c, w_o, j, xc, ada, i, ctx_row, lmg, lmb, alpha, ctx_tm)
        else:
            xl = _pool_mix(xl, ada, i, lat_row, w_pool, ps, j, lmg, lmb, alpha, tm, seq)
            if update_ctx:
                xc = _pool_mix(xc, ada, i, ctx_row, w_pool, ps, j, lmg, lmb, alpha, n_ctx, n_ctx)
        xl = _ffn(xl, ada, i, lat_row_big, w_in, w_out, lfg, lfb, alpha, tm_big)
        if update_ctx:
            xc = _ffn(xc, ada, i, ctx_row, w_in, w_out, lfg, lfb, alpha, min(tm_big, batch * n_ctx))
    return xl.reshape(batch, seq, d)
```

```python
import functools
import math

import numpy as np
import jax
import jax.numpy as jnp
from jax import lax
from jax.experimental import pallas as pl
from jax.experimental.pallas import tpu as pltpu

F32 = jnp.float32
BF16 = jnp.bfloat16

GRID_W = 64
N_HEADS = 16
HEAD_DIM = 128
WIN_ROWS = 8
WIN_COLS = 16
POOL_WINDOWS = (2, 4, 8, 16)
N_ADA = 6
LN_EPS = 1e-5
LOG2E = math.log2(math.e)

Q_ROWS = 4
K_ROWS = Q_ROWS + WIN_ROWS
Q_TOK = Q_ROWS * GRID_W
K_TOK = K_ROWS * GRID_W
N_BIAS_TILES = 18
NA_OUT_CHUNK = 1024
POOL_HALO = 8
COND_ROWS = 8
ROW_CHUNK = 128
QKV_HEADS_PER_TILE = 4
PROJ_ROW_PARTS = 2

VMEM_LIMIT = 60 * 1024 * 1024


def _cparams(*sem):
    return pltpu.CompilerParams(dimension_semantics=sem, vmem_limit_bytes=VMEM_LIMIT)


def _layer_norm(v, g, b):
    mu = jnp.mean(v, axis=-1, keepdims=True)
    d = v - mu
    var = jnp.mean(d * d, axis=-1, keepdims=True)
    return d * lax.rsqrt(var + LN_EPS) * g + b


def _silu(v):
    return v * jax.nn.sigmoid(v)


def _for_row_chunks(n_rows, fn):
    def body(c, carry):
        fn(pl.ds(pl.multiple_of(c * ROW_CHUNK, ROW_CHUNK), ROW_CHUNK))
        return carry
    lax.fori_loop(0, n_rows // ROW_CHUNK, body, 0)


def _modulate_rows(x_ref, sh_ref, sc_ref, h_ref):
    sc1 = 1.0 + sc_ref[...]
    sh = sh_ref[...]

    def chunk(rows):
        h_ref[rows, :] = (x_ref[rows, :] * sc1 + sh).astype(BF16)
    _for_row_chunks(x_ref.shape[0], chunk)


def _cast_side_job(w_in, w_out, layer, n_i, n_j):
    _, d, f2 = w_in.shape
    f = w_out.shape[1]
    bi, bj, ci = d // n_i, f2 // n_j, f // n_j
    in_specs = [pl.BlockSpec((None, bi, bj), lambda i, j: (layer, i, j)),
                pl.BlockSpec((None, ci, bi), lambda i, j: (layer, j, i))]
    out_specs = [pl.BlockSpec((bi, bj), lambda i, j: (i, j)), pl.BlockSpec((ci, bi), lambda i, j: (j, i))]
    out_shape = [jax.ShapeDtypeStruct((d, f2), BF16), jax.ShapeDtypeStruct((f, d), BF16)]
    return in_specs, out_specs, out_shape


def _run_cast_side_job(src_refs, dst_refs):
    for src, dst in zip(src_refs, dst_refs):
        dst[...] = src[...].astype(BF16)


def _ada_kernel(cond_ref, w_ref, b_ref, o_ref):
    s = _silu(cond_ref[...]).astype(BF16)
    o_ref[...] = jnp.dot(s, w_ref[...].astype(BF16), preferred_element_type=F32) + b_ref[...]


def _ada_params(cond, ada_w, ada_b, tn=1024):
    depth, d, n = ada_w.shape
    return pl.pallas_call(
        _ada_kernel,
        grid=(depth, n // tn),
        in_specs=[
            pl.BlockSpec((COND_ROWS, d), lambda l, j: (0, 0)),
            pl.BlockSpec((None, d, tn), lambda l, j: (l, 0, j)),
            pl.BlockSpec((None, 1, tn), lambda l, j: (l, 0, j)),
        ],
        out_specs=pl.BlockSpec((None, COND_ROWS, tn), lambda l, j: (l, 0, j)),
        out_shape=jax.ShapeDtypeStruct((depth, COND_ROWS, n), F32),
        compiler_params=_cparams("parallel", "parallel"),
        name="ada_params",
    )(cond, ada_w, ada_b.reshape(depth, 1, n))


def _ada_spec(layer, which, row_fn, d):
    return pl.BlockSpec((None, None, None, 1, d), lambda i, *_: (layer, row_fn(i), which, 0, 0))


def _vec_spec(layer, d):
    return pl.BlockSpec((None, 1, d), lambda i, *_: (layer, 0, 0))


def _qkv_kernel(x_ref, sh_ref, sc_ref, wq_ref, wk_ref, wv_ref, *rest):
    n_cast = (len(rest) - 4) // 2
    cast_src, (q_ref, k_ref, vt_ref), cast_dst, h_ref = (
        rest[:n_cast], rest[n_cast:n_cast + 3], rest[n_cast + 3:-1], rest[-1])

    @pl.when(pl.program_id(1) == 0)
    def _():
        _modulate_rows(x_ref, sh_ref, sc_ref, h_ref)

    _run_cast_side_job(cast_src, cast_dst)
    h = h_ref[...]
    for w_ref, o_ref, transposed in ((wq_ref, q_ref, False), (wk_ref, k_ref, False), (wv_ref, vt_ref, True)):
        res = jnp.dot(h, w_ref[...], preferred_element_type=F32)
        for hh in range(o_ref.shape[0]):
            part = res[:, hh * HEAD_DIM:(hh + 1) * HEAD_DIM]
            o_ref[hh] = (part.T if transposed else part).astype(BF16)


def _qkv_proj(x, ada, layer, row_fn, w_qkv, j_na, tm, cast=None):
    m, d = x.shape
    hpt = QKV_HEADS_PER_TILE
    tn = hpt * HEAD_DIM
    n_groups = N_HEADS // hpt
    cast_in, cast_out, cast_shape = _cast_side_job(*cast, m // tm, n_groups) if cast else ([], [], [])
    return pl.pallas_call(
        _qkv_kernel,
        grid=(m // tm, n_groups),
        in_specs=[
            pl.BlockSpec((tm, d), lambda i, j: (i, 0)),
            _ada_spec(layer, 0, row_fn, d),
            _ada_spec(layer, 1, row_fn, d),
            pl.BlockSpec((None, d, tn), lambda i, j: (j_na, 0, j)),
            pl.BlockSpec((None, d, tn), lambda i, j: (j_na, 0, n_groups + j)),
            pl.BlockSpec((None, d, tn), lambda i, j: (j_na, 0, 2 * n_groups + j)),
        ] + cast_in,
        out_specs=[
            pl.BlockSpec((hpt, tm, HEAD_DIM), lambda i, j: (j, i, 0)),
            pl.BlockSpec((hpt, tm, HEAD_DIM), lambda i, j: (j, i, 0)),
            pl.BlockSpec((hpt, HEAD_DIM, tm), lambda i, j: (j, 0, i)),
        ] + cast_out,
        out_shape=[
            jax.ShapeDtypeStruct((N_HEADS, m, HEAD_DIM), BF16),
            jax.ShapeDtypeStruct((N_HEADS, m, HEAD_DIM), BF16),
            jax.ShapeDtypeStruct((N_HEADS, HEAD_DIM, m), BF16),
        ] + cast_shape,
        scratch_shapes=[pltpu.VMEM((tm, d), BF16)],
        compiler_params=_cparams("parallel", "arbitrary"),
        name="qkv_proj",
    )(x, ada, ada, w_qkv, w_qkv, w_qkv, *(cast[:2] if cast else ()))


def _bias_tiles(rpb):
    n_dr, n_dc = rpb.shape[1], rpb.shape[2]
    period = 2 * GRID_W + 1
    rev = jnp.pad(rpb[..., ::-1], ((0, 0), (0, 0), (0, period - n_dc)))
    skew = jnp.tile(rev, (1, 1, GRID_W))[..., :GRID_W * (period - 1)].reshape(-1, n_dr, GRID_W, period - 1)
    toep = skew[..., WIN_COLS - 1:WIN_COLS - 1 + GRID_W]
    first = jnp.concatenate([toep, toep[:, n_dr - 1:], toep[:, 3:4], toep[:, 11:12]], axis=1)
    second = jnp.concatenate([toep[:, :1], toep, toep[:, 2:3], toep[:, 10:11]], axis=1)
    vals = jnp.concatenate([first, second], axis=-1) * LOG2E

    e = np.arange(N_BIAS_TILES)[:, None, None]
    ck = np.arange(GRID_W)[None, :, None]
    lane = np.arange(2 * GRID_W)[None, None, :]
    half, cq = lane // GRID_W, lane % GRID_W
    d = np.where(e == 16, -4, np.where(e == 17, 4, e - (WIN_ROWS - 1)))
    dr = d - half
    row_ok = (np.abs(dr) <= WIN_ROWS - 1) & ((e != 16) | (half == 0)) & ((e != 17) | (half == 1))
    ws = np.clip(cq - WIN_COLS // 2, 0, GRID_W - WIN_COLS)
    ok = row_ok & (ck >= ws) & (ck < ws + WIN_COLS)
    return jnp.where(jnp.asarray(ok)[None], vals, -jnp.inf).astype(F32)


def _tile_entry(d):
    return d + WIN_ROWS - 1


_TILES_TOP = [[(j, _tile_entry(j - 2 * u)) for j in range(0, WIN_ROWS)] for u in range(Q_ROWS // 2)]
_TILES_BOT = [[(j, _tile_entry(j - WIN_ROWS - 2 * u)) for j in range(K_ROWS - WIN_ROWS, K_ROWS)]
              for u in range(Q_ROWS // 2)]
_TILES_MID = [[(j, 16 if j - 2 * u - 4 == -4 else 17 if j - 2 * u - 4 == 4 else _tile_entry(j - 2 * u - 4))
               for j in range(2 * u, 2 * u + WIN_ROWS + 1)] for u in range(Q_ROWS // 2)]


def _na_kernel(q_ref, k_ref, vt_ref, kc_ref, vct_ref, tab_ref, o_ref,
               st_ref, pt_ref, sct_ref, pct_ref, otw_ref, linv_ref, *, n_blocks, n_ctx):
    lanes = 2 * GRID_W
    n_ctx_tiles = n_ctx // GRID_W
    n_pairs = Q_ROWS // 2
    last = n_blocks - 1
    zero_tile = jnp.zeros((GRID_W, lanes), BF16)
    nt = (((1,), (1,)), ((), ()))

    def tok(b):
        return b * Q_TOK if isinstance(b, int) else pl.multiple_of(b * Q_TOK, Q_TOK)

    def key_block(b):
        hi = n_blocks - K_ROWS // Q_ROWS
        return min(max(b - 1, 0), hi) if isinstance(b, int) else jnp.clip(b - 1, 0, hi)

    def zero_unused(pslot, tiles):
        for u in range(n_pairs):
            used = dict(tiles[u])
            for j in range(K_ROWS):
                if j not in used:
                    pt_ref[pslot, j * GRID_W:(j + 1) * GRID_W, u * lanes:(u + 1) * lanes] = zero_tile

    def scores(b, sslot):
        q = q_ref[pl.ds(tok(b), Q_TOK), :]
        st_ref[sslot] = lax.dot_general(k_ref[pl.ds(tok(key_block(b)), K_TOK), :], q, nt,
                                        preferred_element_type=F32)

    def softmax(b, sslot, pslot, tiles):
        for u in range(n_pairs):
            cols = slice(u * lanes, (u + 1) * lanes)
            qcols = pl.ds(tok(b) + u * lanes, lanes)

            def tile_scores(i):
                if i < len(tiles[u]):
                    j, e = tiles[u][i]
                    return st_ref[sslot, j * GRID_W:(j + 1) * GRID_W, cols] + tab_ref[e]
                t = i - len(tiles[u])
                return sct_ref[t * GRID_W:(t + 1) * GRID_W, qcols]

            n_tiles = len(tiles[u]) + n_ctx_tiles
            m = functools.reduce(jnp.maximum, [tile_scores(i) for i in range(n_tiles)])
            m = jnp.max(m, axis=0, keepdims=True)
            l = None
            for i in range(n_tiles):
                p = jnp.exp2(tile_scores(i) - m)
                l = p if l is None else l + p
                if i < len(tiles[u]):
                    j = tiles[u][i][0]
                    pt_ref[pslot, j * GRID_W:(j + 1) * GRID_W, cols] = p.astype(BF16)
                else:
                    t = i - len(tiles[u])
                    pct_ref[t * GRID_W:(t + 1) * GRID_W, qcols] = p.astype(BF16)
            linv_ref[0:1, qcols] = 1.0 / jnp.sum(l, axis=0, keepdims=True)

    def values(b, pslot):
        otw_ref[:, pl.ds(tok(b), Q_TOK)] = jnp.dot(vt_ref[:, pl.ds(tok(key_block(b)), K_TOK)], pt_ref[pslot],
                                                   preferred_element_type=F32)

    zero_unused(0, _TILES_MID)
    zero_unused(1, _TILES_MID)
    zero_unused(2, _TILES_TOP)
    zero_unused(3, _TILES_BOT)
    sct_ref[...] = lax.dot_general(kc_ref[...], q_ref[...], nt, preferred_element_type=F32)

    scores(0, 0)
    scores(1, 1)
    softmax(0, 0, 2, _TILES_TOP)
    scores(2, 0)
    softmax(1, 1, 1, _TILES_MID)
    values(0, 2)

    def body(t, carry):
        b = 2 * t
        scores(b + 1, 1)
        softmax(b, 0, 0, _TILES_MID)
        values(b - 1, 1)
        scores(b + 2, 0)
        softmax(b + 1, 1, 1, _TILES_MID)
        values(b, 0)
        return carry

    lax.fori_loop(1, n_blocks // 2 - 1, body, 0)
    scores(last, 1)
    softmax(last - 1, 0, 0, _TILES_MID)
    values(last - 2, 1)
    softmax(last, 1, 3, _TILES_BOT)
    values(last - 1, 0)
    values(last, 3)

    for c in range(n_blocks * Q_TOK // NA_OUT_CHUNK):
        cols = slice(c * NA_OUT_CHUNK, (c + 1) * NA_OUT_CHUNK)
        ot = otw_ref[:, cols] + jnp.dot(vct_ref[...], pct_ref[:, cols], preferred_element_type=F32)
        o_ref[cols, :] = (ot * linv_ref[0:1, cols]).T.astype(BF16)


def _neighbourhood_attention(q, k, vt, k_c, vt_c, tab, batch):
    m = q.shape[1]
    seq = m // batch
    n_ctx = k_c.shape[1] // batch
    n_blocks = seq // Q_TOK
    assert n_blocks % 2 == 0 and n_blocks >= 4 and seq % NA_OUT_CHUNK == 0
    kern = functools.partial(_na_kernel, n_blocks=n_blocks, n_ctx=n_ctx)
    return pl.pallas_call(
        kern,
        grid=(batch, N_HEADS),
        in_specs=[
            pl.BlockSpec((None, seq, HEAD_DIM), lambda b, h: (h, b, 0)),
            pl.BlockSpec((None, seq, HEAD_DIM), lambda b, h: (h, b, 0)),
            pl.BlockSpec((None, HEAD_DIM, seq), lambda b, h: (h, 0, b)),
            pl.BlockSpec((None, n_ctx, HEAD_DIM), lambda b, h: (h, b, 0)),
            pl.BlockSpec((None, HEAD_DIM, n_ctx), lambda b, h: (h, 0, b)),
            pl.BlockSpec((None, N_BIAS_TILES, GRID_W, 2 * GRID_W), lambda b, h: (h, 0, 0, 0)),
        ],
        out_specs=pl.BlockSpec((None, seq, HEAD_DIM), lambda b, h: (h, b, 0)),
        out_shape=jax.ShapeDtypeStruct((N_HEADS, m, HEAD_DIM), BF16),
        scratch_shapes=[
            pltpu.VMEM((2, K_TOK, Q_TOK), F32),
            pltpu.VMEM((4, K_TOK, Q_TOK), BF16),
            pltpu.VMEM((n_ctx, seq), F32),
            pltpu.VMEM((n_ctx, seq), BF16),
            pltpu.VMEM((HEAD_DIM, seq), F32),
            pltpu.VMEM((8, seq), F32),
        ],
        compiler_params=_cparams("parallel", "parallel"),
        name="neighbourhood_attention",
    )(q, k, vt, k_c, vt_c, tab)


def _ctx_attn_kernel(q_ref, k_ref, vt_ref, o_ref):
    nt = (((1,), (1,)), ((), ()))
    st = lax.dot_general(k_ref[...], q_ref[...], nt, preferred_element_type=F32)
    p = jnp.exp2(st - jnp.max(st, axis=0, keepdims=True))
    inv_l = 1.0 / jnp.sum(p, axis=0, keepdims=True)
    ot = jnp.dot(vt_ref[...], p.astype(BF16), preferred_element_type=F32) * inv_l
    o_ref[...] = ot.T.astype(BF16)


def _context_attention(q_c, k_c, vt_c, batch):
    m = q_c.shape[1]
    n_ctx = m // batch
    return pl.pallas_call(
        _ctx_attn_kernel,
        grid=(batch, N_HEADS),
        in_specs=[
            pl.BlockSpec((None, n_ctx, HEAD_DIM), lambda b, h: (h, b, 0)),
            pl.BlockSpec((None, n_ctx, HEAD_DIM), lambda b, h: (h, b, 0)),
            pl.BlockSpec((None, HEAD_DIM, n_ctx), lambda b, h: (h, 0, b)),
        ],
        out_specs=pl.BlockSpec((None, n_ctx, HEAD_DIM), lambda b, h: (h, b, 0)),
        out_shape=jax.ShapeDtypeStruct((N_HEADS, m, HEAD_DIM), BF16),
        compiler_params=_cparams("parallel", "parallel"),
        name="context_attention",
    )(q_c, k_c, vt_c)


def _proj_ln_kernel(a_ref, w_ref, x_ref, g_ref, lng_ref, lnb_ref, o_ref, acat_ref, *, alpha):
    for h in range(N_HEADS):
        acat_ref[:, h * HEAD_DIM:(h + 1) * HEAD_DIM] = a_ref[h]
    part = a_ref.shape[1] // PROJ_ROW_PARTS
    for r in range(PROJ_ROW_PARTS):
        rows = slice(r * part, (r + 1) * part)
        y = jnp.dot(acat_ref[rows, :], w_ref[...], preferred_element_type=F32)
        o_ref[rows, :] = _layer_norm(alpha * x_ref[rows, :] + g_ref[...] * y, lng_ref[...], lnb_ref[...])


def _proj_ln(a, w_o, j_na, x, ada, layer, row_fn, ln_g, ln_b, alpha, tm):
    m, d = x.shape
    return pl.pallas_call(
        functools.partial(_proj_ln_kernel, alpha=alpha),
        grid=(m // tm,),
        in_specs=[
            pl.BlockSpec((N_HEADS, tm, HEAD_DIM), lambda i: (0, i, 0)),
            pl.BlockSpec((None, d, d), lambda i: (j_na, 0, 0)),
            pl.BlockSpec((tm, d), lambda i: (i, 0)),
            _ada_spec(layer, 2, row_fn, d),
            _vec_spec(layer, d),
            _vec_spec(layer, d),
        ],
        out_specs=pl.BlockSpec((tm, d), lambda i: (i, 0)),
        out_shape=jax.ShapeDtypeStruct((m, d), F32),
        scratch_shapes=[pltpu.VMEM((tm, d), BF16)],
        compiler_params=_cparams("parallel"),
        name="out_proj_norm",
    )(a, w_o, x, ada, ln_g, ln_b)


def _pool_kernel(x_ref, xp_ref, xn_ref, sh_ref, sc_ref, g_ref, w_ref, ps_ref, lng_ref, lnb_ref, o_ref, ext_ref,
                 *, alpha, seq_len):
    tm, d = x_ref.shape
    gw = d // len(POOL_WINDOWS)
    start = pl.program_id(0) * tm
    pos0 = start % seq_len
    sc1 = 1.0 + sc_ref[...]
    sh = sh_ref[...]
    h = x_ref[...] * sc1 + sh
    ext_ref[0:POOL_HALO, :] = jnp.where(pos0 > 0, xp_ref[...] * sc1 + sh, 0.0)
    ext_ref[POOL_HALO:POOL_HALO + tm, :] = h
    ext_ref[POOL_HALO + tm:, :] = jnp.where(pos0 + tm < seq_len, xn_ref[...] * sc1 + sh, 0.0)
    n_ext = tm + 2 * POOL_HALO
    pos = pos0 + lax.broadcasted_iota(jnp.int32, (tm, 128), 0)
    ys = []
    for gi, w in enumerate(POOL_WINDOWS):
        cols = slice(gi * gw, (gi + 1) * gw)
        ext = ext_ref[:, cols]
        s = ext + pltpu.roll(ext, 1, axis=0)
        half = 1
        while 2 * half < w:
            s = pltpu.roll(s, half, axis=0) + pltpu.roll(s, n_ext - half, axis=0)
            half *= 2
        tot = s[POOL_HALO:POOL_HALO + tm]
        cnt = jnp.minimum(pos - w // 2 + w, seq_len) - jnp.maximum(pos - w // 2, 0)
        inv_cnt = jnp.tile(1.0 / cnt.astype(F32), (1, gw // 128))
        pooled = tot * inv_cnt - h[:, cols]
        ys.append(jnp.dot(pooled.astype(BF16), w_ref[gi], preferred_element_type=F32))
    y = jnp.concatenate(ys, axis=1) * ps_ref[...]
    o_ref[...] = _layer_norm(alpha * x_ref[...] + g_ref[...] * y, lng_ref[...], lnb_ref[...])


def _pool_mix(x, ada, layer, row_fn, w_pool, pool_scale, j_pool, ln_g, ln_b, alpha, tm, seq_len):
    m, d = x.shape
    hb = tm // POOL_HALO
    n_halo_blocks = m // POOL_HALO
    gw = d // len(POOL_WINDOWS)
    return pl.pallas_call(
        functools.partial(_pool_kernel, alpha=alpha, seq_len=seq_len),
        grid=(m // tm,),
        in_specs=[
            pl.BlockSpec((tm, d), lambda i: (i, 0)),
            pl.BlockSpec((POOL_HALO, d), lambda i: (jnp.maximum(i * hb - 1, 0), 0)),
            pl.BlockSpec((POOL_HALO, d), lambda i: (jnp.minimum((i + 1) * hb, n_halo_blocks - 1), 0)),
            _ada_spec(layer, 0, row_fn, d),
            _ada_spec(layer, 1, row_fn, d),
            _ada_spec(layer, 2, row_fn, d),
            pl.BlockSpec((None, len(POOL_WINDOWS), gw, gw), lambda i: (j_pool, 0, 0, 0)),
            _vec_spec(j_pool, d),
            _vec_spec(layer, d),
            _vec_spec(layer, d),
        ],
        out_specs=pl.BlockSpec((tm, d), lambda i: (i, 0)),
        out_shape=jax.ShapeDtypeStruct((m, d), F32),
        scratch_shapes=[pltpu.VMEM((tm + 2 * POOL_HALO, d), F32)],
        compiler_params=_cparams("parallel"),
        name="pool_mix_norm",
    )(x, x, x, ada, ada, ada, w_pool, pool_scale, ln_g, ln_b)


def _ffn_kernel(x_ref, sh_ref, sc_ref, g_ref, wg_ref, wu_ref, wo_ref, lng_ref, lnb_ref, *rest, alpha):
    n_cast = (len(rest) - 2) // 2
    cast_src, o_ref, cast_dst, h_ref = rest[:n_cast], rest[n_cast], rest[n_cast + 1:-1], rest[-1]
    f = pl.program_id(1)

    @pl.when(f == 0)
    def _():
        _modulate_rows(x_ref, sh_ref, sc_ref, h_ref)
        o_ref[...] = jnp.zeros_like(o_ref)

    _run_cast_side_job(cast_src, cast_dst)
    h = h_ref[...]
    gate = jnp.dot(h, wg_ref[...], preferred_element_type=F32)
    up = jnp.dot(h, wu_ref[...], preferred_element_type=F32)
    o_ref[...] += jnp.dot((_silu(gate) * up).astype(BF16), wo_ref[...], preferred_element_type=F32)

    @pl.when(f == pl.num_programs(1) - 1)
    def _():
        g, lng, lnb = g_ref[...], lng_ref[...], lnb_ref[...]

        def chunk(rows):
            o_ref[rows, :] = _layer_norm(alpha * x_ref[rows, :] + g * o_ref[rows, :], lng, lnb)
        _for_row_chunks(o_ref.shape[0], chunk)


def _ffn(x, ada, layer, row_fn, w_in, w_out, ln_g, ln_b, alpha, tm, tf=512, cast=None):
    m, d = x.shape
    d_ff = w_out.shape[0]
    nf = d_ff // tf
    cast_in, cast_out, cast_shape = _cast_side_job(*cast, m // tm, nf) if cast else ([], [], [])
    return pl.pallas_call(
        functools.partial(_ffn_kernel, alpha=alpha),
        grid=(m // tm, nf),
        in_specs=[
            pl.BlockSpec((tm, d), lambda i, f: (i, 0)),
            _ada_spec(layer, 3, row_fn, d),
            _ada_spec(layer, 4, row_fn, d),
            _ada_spec(layer, 5, row_fn, d),
            pl.BlockSpec((d, tf), lambda i, f: (0, f)),
            pl.BlockSpec((d, tf), lambda i, f: (0, nf + f)),
            pl.BlockSpec((tf, d), lambda i, f: (f, 0)),
            _vec_spec(layer, d),
            _vec_spec(layer, d),
        ] + cast_in,
        out_specs=[pl.BlockSpec((tm, d), lambda i, f: (i, 0))] + cast_out,
        out_shape=[jax.ShapeDtypeStruct((m, d), F32)] + cast_shape,
        scratch_shapes=[pltpu.VMEM((tm, d), BF16)],
        compiler_params=_cparams("parallel", "arbitrary"),
        name="ffn_norm",
    )(x, ada, ada, ada, w_in, w_in, w_out, ln_g, ln_b, *(cast[:2] if cast else ()))


def kernel(x, c, ctx, c_ctx, ada_w, ada_b, ln_mix_g, ln_mix_b, ln_ffn_g, ln_ffn_b,
           na_w_qkv, na_w_o, na_rpb, pool_w, pool_scale, ffn_w_in, ffn_w_out):
    batch, seq, d = x.shape
    n_ctx = ctx.shape[1]
    depth = ada_w.shape[0]
    n_mixers = 2
    assert seq % Q_TOK == 0 and seq == GRID_W * GRID_W and batch < COND_ROWS
    alpha = (2 * depth) ** 0.25
    last_na = max(i for i in range(depth) if i % n_mixers == 0)

    cond = jnp.concatenate([c, c_ctx[None], jnp.zeros((COND_ROWS - batch - 1, d), F32)], axis=0)
    ada = _ada_params(cond, ada_w, ada_b).reshape(depth, COND_ROWS, N_ADA, 1, d)

    tm = 512
    tm_big = 1024
    lat_row = lambda i: i // (seq // tm)
    lat_row_big = lambda i: i // (seq // tm_big)
    ctx_row = lambda i: batch
    ctx_tm = min(tm, batch * n_ctx)

    qscale = jnp.concatenate([jnp.full((d,), HEAD_DIM ** -0.5 * LOG2E, F32), jnp.ones((2 * d,), F32)])
    w_qkv = (na_w_qkv * qscale).astype(BF16)
    w_o = na_w_o.astype(BF16)
    w_pool = pool_w.astype(BF16)
    w_in = [None] * depth
    w_out = [None] * depth
    lmg, lmb = ln_mix_g.reshape(depth, 1, d), ln_mix_b.reshape(depth, 1, d)
    lfg, lfb = ln_ffn_g.reshape(depth, 1, d), ln_ffn_b.reshape(depth, 1, d)
    ps = pool_scale.reshape(-1, 1, d)

    xl = x.reshape(batch * seq, d)
    xc = ctx.reshape(batch * n_ctx, d)
    for i in range(depth):
        use_na = i % n_mixers == 0
        j = i // n_mixers
        update_ctx = i < last_na
        if use_na:
            cast = (ffn_w_in, ffn_w_out, i) if w_in[i] is None else None
            q, k, vt, *converted = _qkv_proj(xl, ada, i, lat_row_big, w_qkv, j, tm_big, cast=cast)
            if converted:
                w_in[i], w_out[i] = converted
            q_c, k_c, vt_c = _qkv_proj(xc, ada, i, ctx_row, w_qkv, j, batch * n_ctx)
            att = _neighbourhood_attention(q, k, vt, k_c, vt_c, _bias_tiles(na_rpb[j]), batch)
            xl = _proj_ln(att, w_o, j, xl, ada, i, lat_row, lmg, lmb, alpha, tm)
            if update_ctx:
                att_c = _context_attention(q_c, k_c, vt_c, batch)
                xc = _proj_ln(att_c, w_o, j, xc, ada, i, ctx_row, lmg, lmb, alpha, ctx_tm)
        else:
            xl = _pool_mix(xl, ada, i, lat_row, w_pool, ps, j, lmg, lmb, alpha, tm, seq)
            if update_ctx:
                xc = _pool_mix(xc, ada, i, ctx_row, w_pool, ps, j, lmg, lmb, alpha, n_ctx, n_ctx)
        if w_in[i] is None:
            w_in[i], w_out[i] = ffn_w_in[i].astype(BF16), ffn_w_out[i].astype(BF16)
        cast = (ffn_w_in, ffn_w_out, i + 1) if i + 1 < depth else None
        xl, *converted = _ffn(xl, ada, i, lat_row_big, w_in[i], w_out[i], lfg, lfb, alpha, tm_big, cast=cast)
        if converted:
            w_in[i + 1], w_out[i + 1] = converted
        if update_ctx:
            xc, = _ffn(xc, ada, i, ctx_row, w_in[i], w_out[i], lfg, lfb, alpha, min(tm_big, batch * n_ctx))
    return xl.reshape(batch, seq, d)
```

```python
import functools
import math

import numpy as np
import jax
import jax.numpy as jnp
from jax import lax
from jax.experimental import pallas as pl
from jax.experimental.pallas import tpu as pltpu

F32 = jnp.float32
BF16 = jnp.bfloat16

GRID_W = 64
N_HEADS = 16
HEAD_DIM = 128
WIN_ROWS = 8
WIN_COLS = 16
POOL_WINDOWS = (2, 4, 8, 16)
N_ADA = 6
LN_EPS = 1e-5
LOG2E = math.log2(math.e)

Q_ROWS = 4
K_ROWS = Q_ROWS + WIN_ROWS
Q_TOK = Q_ROWS * GRID_W
K_TOK = K_ROWS * GRID_W
N_BIAS_TILES = 18
NA_OUT_CHUNK = 1024
POOL_HALO = 8
COND_ROWS = 8
ROW_CHUNK = 128
QKV_HEADS_PER_TILE = 4

VMEM_LIMIT = 60 * 1024 * 1024


def _cparams(*sem):
    return pltpu.CompilerParams(dimension_semantics=sem, vmem_limit_bytes=VMEM_LIMIT)


def _layer_norm(v, g, b):
    mu = jnp.mean(v, axis=-1, keepdims=True)
    d = v - mu
    var = jnp.mean(d * d, axis=-1, keepdims=True)
    return d * lax.rsqrt(var + LN_EPS) * g + b


def _silu(v):
    return v * jax.nn.sigmoid(v)


def _for_row_chunks(n_rows, fn):
    def body(c, carry):
        fn(pl.ds(pl.multiple_of(c * ROW_CHUNK, ROW_CHUNK), ROW_CHUNK))
        return carry
    lax.fori_loop(0, n_rows // ROW_CHUNK, body, 0)


def _modulate_rows(x_ref, sh_ref, sc_ref, h_ref):
    sc1 = 1.0 + sc_ref[...]
    sh = sh_ref[...]

    def chunk(rows):
        h_ref[rows, :] = (x_ref[rows, :] * sc1 + sh).astype(BF16)
    _for_row_chunks(x_ref.shape[0], chunk)


def _cast_side_job(w_in, w_out, layer, n_i, n_j):
    _, d, f2 = w_in.shape
    f = w_out.shape[1]
    bi, bj, ci = d // n_i, f2 // n_j, f // n_j
    in_specs = [pl.BlockSpec((None, bi, bj), lambda i, j: (layer, i, j)),
                pl.BlockSpec((None, ci, bi), lambda i, j: (layer, j, i))]
    out_specs = [pl.BlockSpec((bi, bj), lambda i, j: (i, j)), pl.BlockSpec((ci, bi), lambda i, j: (j, i))]
    out_shape = [jax.ShapeDtypeStruct((d, f2), BF16), jax.ShapeDtypeStruct((f, d), BF16)]
    return in_specs, out_specs, out_shape


def _run_cast_side_job(src_refs, dst_refs):
    for src, dst in zip(src_refs, dst_refs):
        dst[...] = src[...].astype(BF16)


def _ada_kernel(cond_ref, w_ref, b_ref, o_ref):
    s = _silu(cond_ref[...]).astype(BF16)
    o_ref[...] = jnp.dot(s, w_ref[...].astype(BF16), preferred_element_type=F32) + b_ref[...]


def _ada_params(cond, ada_w, ada_b, tn=1024):
    depth, d, n = ada_w.shape
    return pl.pallas_call(
        _ada_kernel,
        grid=(depth, n // tn),
        in_specs=[
            pl.BlockSpec((COND_ROWS, d), lambda l, j: (0, 0)),
            pl.BlockSpec((None, d, tn), lambda l, j: (l, 0, j)),
            pl.BlockSpec((None, 1, tn), lambda l, j: (l, 0, j)),
        ],
        out_specs=pl.BlockSpec((None, COND_ROWS, tn), lambda l, j: (l, 0, j)),
        out_shape=jax.ShapeDtypeStruct((depth, COND_ROWS, n), F32),
        compiler_params=_cparams("parallel", "parallel"),
        name="ada_params",
    )(cond, ada_w, ada_b.reshape(depth, 1, n))


def _ada_spec(layer, which, row_fn, d):
    return pl.BlockSpec((None, None, None, 1, d), lambda i, *_: (layer, row_fn(i), which, 0, 0))


def _vec_spec(layer, d):
    return pl.BlockSpec((None, 1, d), lambda i, *_: (layer, 0, 0))


def _qkv_kernel(x_ref, sh_ref, sc_ref, wq_ref, wk_ref, wv_ref, *rest):
    n_cast = (len(rest) - 4) // 2
    cast_src, (q_ref, k_ref, vt_ref), cast_dst, h_ref = (
        rest[:n_cast], rest[n_cast:n_cast + 3], rest[n_cast + 3:-1], rest[-1])

    @pl.when(pl.program_id(1) == 0)
    def _():
        _modulate_rows(x_ref, sh_ref, sc_ref, h_ref)

    _run_cast_side_job(cast_src, cast_dst)
    h = h_ref[...]
    for w_ref, o_ref, transposed in ((wq_ref, q_ref, False), (wk_ref, k_ref, False), (wv_ref, vt_ref, True)):
        res = jnp.dot(h, w_ref[...], preferred_element_type=F32)
        for hh in range(o_ref.shape[0]):
            part = res[:, hh * HEAD_DIM:(hh + 1) * HEAD_DIM]
            o_ref[hh] = (part.T if transposed else part).astype(BF16)


def _qkv_proj(x, ada, layer, row_fn, w_qkv, j_na, tm, cast=None):
    m, d = x.shape
    hpt = QKV_HEADS_PER_TILE
    tn = hpt * HEAD_DIM
    n_groups = N_HEADS // hpt
    cast_in, cast_out, cast_shape = _cast_side_job(*cast, m // tm, n_groups) if cast else ([], [], [])
    return pl.pallas_call(
        _qkv_kernel,
        grid=(m // tm, n_groups),
        in_specs=[
            pl.BlockSpec((tm, d), lambda i, j: (i, 0)),
            _ada_spec(layer, 0, row_fn, d),
            _ada_spec(layer, 1, row_fn, d),
            pl.BlockSpec((None, d, tn), lambda i, j: (j_na, 0, j)),
            pl.BlockSpec((None, d, tn), lambda i, j: (j_na, 0, n_groups + j)),
            pl.BlockSpec((None, d, tn), lambda i, j: (j_na, 0, 2 * n_groups + j)),
        ] + cast_in,
        out_specs=[
            pl.BlockSpec((hpt, tm, HEAD_DIM), lambda i, j: (j, i, 0)),
            pl.BlockSpec((hpt, tm, HEAD_DIM), lambda i, j: (j, i, 0)),
            pl.BlockSpec((hpt, HEAD_DIM, tm), lambda i, j: (j, 0, i)),
        ] + cast_out,
        out_shape=[
            jax.ShapeDtypeStruct((N_HEADS, m, HEAD_DIM), BF16),
            jax.ShapeDtypeStruct((N_HEADS, m, HEAD_DIM), BF16),
            jax.ShapeDtypeStruct((N_HEADS, HEAD_DIM, m), BF16),
        ] + cast_shape,
        scratch_shapes=[pltpu.VMEM((tm, d), BF16)],
        compiler_params=_cparams("parallel", "arbitrary"),
        name="qkv_proj",
    )(x, ada, ada, w_qkv, w_qkv, w_qkv, *(cast[:2] if cast else ()))


def _bias_tiles(rpb):
    n_dr, n_dc = rpb.shape[1], rpb.shape[2]
    period = 2 * GRID_W + 1
    rev = jnp.pad(rpb[..., ::-1], ((0, 0), (0, 0), (0, period - n_dc)))
    skew = jnp.tile(rev, (1, 1, GRID_W))[..., :GRID_W * (period - 1)].reshape(-1, n_dr, GRID_W, period - 1)
    toep = skew[..., WIN_COLS - 1:WIN_COLS - 1 + GRID_W]
    first = jnp.concatenate([toep, toep[:, n_dr - 1:], toep[:, 3:4], toep[:, 11:12]], axis=1)
    second = jnp.concatenate([toep[:, :1], toep, toep[:, 2:3], toep[:, 10:11]], axis=1)
    vals = jnp.concatenate([first, second], axis=-1) * LOG2E

    e = np.arange(N_BIAS_TILES)[:, None, None]
    ck = np.arange(GRID_W)[None, :, None]
    lane = np.arange(2 * GRID_W)[None, None, :]
    half, cq = lane // GRID_W, lane % GRID_W
    d = np.where(e == 16, -4, np.where(e == 17, 4, e - (WIN_ROWS - 1)))
    dr = d - half
    row_ok = (np.abs(dr) <= WIN_ROWS - 1) & ((e != 16) | (half == 0)) & ((e != 17) | (half == 1))
    ws = np.clip(cq - WIN_COLS // 2, 0, GRID_W - WIN_COLS)
    ok = row_ok & (ck >= ws) & (ck < ws + WIN_COLS)
    return jnp.where(jnp.asarray(ok)[None], vals, -jnp.inf).astype(F32)


def _tile_entry(d):
    return d + WIN_ROWS - 1


_TILES_TOP = [[(j, _tile_entry(j - 2 * u)) for j in range(0, WIN_ROWS)] for u in range(Q_ROWS // 2)]
_TILES_BOT = [[(j, _tile_entry(j - WIN_ROWS - 2 * u)) for j in range(K_ROWS - WIN_ROWS, K_ROWS)]
              for u in range(Q_ROWS // 2)]
_TILES_MID = [[(j, 16 if j - 2 * u - 4 == -4 else 17 if j - 2 * u - 4 == 4 else _tile_entry(j - 2 * u - 4))
               for j in range(2 * u, 2 * u + WIN_ROWS + 1)] for u in range(Q_ROWS // 2)]


def _na_kernel(q_ref, k_ref, vt_ref, kc_ref, vct_ref, tab_ref, o_ref,
               st_ref, pt_ref, sct_ref, pct_ref, otw_ref, linv_ref, *, n_blocks, n_ctx):
    lanes = 2 * GRID_W
    n_ctx_tiles = n_ctx // GRID_W
    n_pairs = Q_ROWS // 2
    last = n_blocks - 1
    zero_tile = jnp.zeros((GRID_W, lanes), BF16)
    nt = (((1,), (1,)), ((), ()))

    def tok(b):
        return b * Q_TOK if isinstance(b, int) else pl.multiple_of(b * Q_TOK, Q_TOK)

    def key_block(b):
        hi = n_blocks - K_ROWS // Q_ROWS
        return min(max(b - 1, 0), hi) if isinstance(b, int) else jnp.clip(b - 1, 0, hi)

    def zero_unused(pslot, tiles):
        for u in range(n_pairs):
            used = dict(tiles[u])
            for j in range(K_ROWS):
                if j not in used:
                    pt_ref[pslot, j * GRID_W:(j + 1) * GRID_W, u * lanes:(u + 1) * lanes] = zero_tile

    def scores(b, sslot):
        q = q_ref[pl.ds(tok(b), Q_TOK), :]
        st_ref[sslot] = lax.dot_general(k_ref[pl.ds(tok(key_block(b)), K_TOK), :], q, nt,
                                        preferred_element_type=F32)

    def softmax(b, sslot, pslot, tiles):
        for u in range(n_pairs):
            cols = slice(u * lanes, (u + 1) * lanes)
            qcols = pl.ds(tok(b) + u * lanes, lanes)

            def tile_scores(i):
                if i < len(tiles[u]):
                    j, e = tiles[u][i]
                    return st_ref[sslot, j * GRID_W:(j + 1) * GRID_W, cols] + tab_ref[e]
                t = i - len(tiles[u])
                return sct_ref[t * GRID_W:(t + 1) * GRID_W, qcols]

            n_tiles = len(tiles[u]) + n_ctx_tiles
            m = functools.reduce(jnp.maximum, [tile_scores(i) for i in range(n_tiles)])
            m = jnp.max(m, axis=0, keepdims=True)
            l = None
            for i in range(n_tiles):
                p = jnp.exp2(tile_scores(i) - m)
                l = p if l is None else l + p
                if i < len(tiles[u]):
                    j = tiles[u][i][0]
                    pt_ref[pslot, j * GRID_W:(j + 1) * GRID_W, cols] = p.astype(BF16)
                else:
                    t = i - len(tiles[u])
                    pct_ref[t * GRID_W:(t + 1) * GRID_W, qcols] = p.astype(BF16)
            linv_ref[0:1, qcols] = 1.0 / jnp.sum(l, axis=0, keepdims=True)

    def values(b, pslot):
        otw_ref[:, pl.ds(tok(b), Q_TOK)] = jnp.dot(vt_ref[:, pl.ds(tok(key_block(b)), K_TOK)], pt_ref[pslot],
                                                   preferred_element_type=F32)

    zero_unused(0, _TILES_MID)
    zero_unused(1, _TILES_MID)
    zero_unused(2, _TILES_TOP)
    zero_unused(3, _TILES_BOT)
    sct_ref[...] = lax.dot_general(kc_ref[...], q_ref[...], nt, preferred_element_type=F32)

    scores(0, 0)
    scores(1, 1)
    softmax(0, 0, 2, _TILES_TOP)
    scores(2, 0)
    softmax(1, 1, 1, _TILES_MID)
    values(0, 2)

    finished = [0]

    def finish(n_done):
        while (finished[0] + 1) * NA_OUT_CHUNK <= n_done * Q_TOK:
            cols = slice(finished[0] * NA_OUT_CHUNK, (finished[0] + 1) * NA_OUT_CHUNK)
            ot = otw_ref[:, cols] + jnp.dot(vct_ref[...], pct_ref[:, cols], preferred_element_type=F32)
            o_ref[cols, :] = (ot * linv_ref[0:1, cols]).T.astype(BF16)
            finished[0] += 1

    for b in range(2, last - 1, 2):
        scores(b + 1, 1)
        softmax(b, 0, 0, _TILES_MID)
        values(b - 1, 1)
        finish(b)
        scores(b + 2, 0)
        softmax(b + 1, 1, 1, _TILES_MID)
        values(b, 0)
        finish(b + 1)
    scores(last, 1)
    softmax(last - 1, 0, 0, _TILES_MID)
    values(last - 2, 1)
    finish(last - 1)
    softmax(last, 1, 3, _TILES_BOT)
    values(last - 1, 0)
    values(last, 3)
    finish(n_blocks)


def _neighbourhood_attention(q, k, vt, k_c, vt_c, tab, j_na, batch):
    m = q.shape[1]
    seq = m // batch
    n_ctx = k_c.shape[1] // batch
    n_blocks = seq // Q_TOK
    assert n_blocks % 2 == 0 and n_blocks >= 4 and seq % NA_OUT_CHUNK == 0
    kern = functools.partial(_na_kernel, n_blocks=n_blocks, n_ctx=n_ctx)
    return pl.pallas_call(
        kern,
        grid=(N_HEADS, batch),
        in_specs=[
            pl.BlockSpec((None, seq, HEAD_DIM), lambda h, b: (h, b, 0)),
            pl.BlockSpec((None, seq, HEAD_DIM), lambda h, b: (h, b, 0)),
            pl.BlockSpec((None, HEAD_DIM, seq), lambda h, b: (h, 0, b)),
            pl.BlockSpec((None, n_ctx, HEAD_DIM), lambda h, b: (h, b, 0)),
            pl.BlockSpec((None, HEAD_DIM, n_ctx), lambda h, b: (h, 0, b)),
            pl.BlockSpec((None, N_BIAS_TILES, GRID_W, 2 * GRID_W), lambda h, b: (j_na * N_HEADS + h, 0, 0, 0)),
        ],
        out_specs=pl.BlockSpec((None, seq, HEAD_DIM), lambda h, b: (h, b, 0)),
        out_shape=jax.ShapeDtypeStruct((N_HEADS, m, HEAD_DIM), BF16),
        scratch_shapes=[
            pltpu.VMEM((2, K_TOK, Q_TOK), F32),
            pltpu.VMEM((4, K_TOK, Q_TOK), BF16),
            pltpu.VMEM((n_ctx, seq), F32),
            pltpu.VMEM((n_ctx, seq), BF16),
            pltpu.VMEM((HEAD_DIM, seq), F32),
            pltpu.VMEM((8, seq), F32),
        ],
        compiler_params=_cparams("parallel", "parallel"),
        name="neighbourhood_attention",
    )(q, k, vt, k_c, vt_c, tab)


def _ctx_attn_kernel(q_ref, k_ref, vt_ref, o_ref):
    nt = (((1,), (1,)), ((), ()))
    for h in range(q_ref.shape[0]):
        st = lax.dot_general(k_ref[h], q_ref[h], nt, preferred_element_type=F32)
        p = jnp.exp2(st - jnp.max(st, axis=0, keepdims=True))
        inv_l = 1.0 / jnp.sum(p, axis=0, keepdims=True)
        ot = jnp.dot(vt_ref[h], p.astype(BF16), preferred_element_type=F32) * inv_l
        o_ref[h] = ot.T.astype(BF16)


def _context_attention(q_c, k_c, vt_c, batch):
    m = q_c.shape[1]
    n_ctx = m // batch
    return pl.pallas_call(
        _ctx_attn_kernel,
        grid=(batch,),
        in_specs=[
            pl.BlockSpec((N_HEADS, n_ctx, HEAD_DIM), lambda b: (0, b, 0)),
            pl.BlockSpec((N_HEADS, n_ctx, HEAD_DIM), lambda b: (0, b, 0)),
            pl.BlockSpec((N_HEADS, HEAD_DIM, n_ctx), lambda b: (0, 0, b)),
        ],
        out_specs=pl.BlockSpec((N_HEADS, n_ctx, HEAD_DIM), lambda b: (0, b, 0)),
        out_shape=jax.ShapeDtypeStruct((N_HEADS, m, HEAD_DIM), BF16),
        compiler_params=_cparams("parallel"),
        name="context_attention",
    )(q_c, k_c, vt_c)


def _proj_ln_kernel(a_ref, w_ref, x_ref, g_ref, lng_ref, lnb_ref, o_ref, acat_ref, *, alpha):
    for h in range(N_HEADS):
        acat_ref[:, h * HEAD_DIM:(h + 1) * HEAD_DIM] = a_ref[h]
    y = jnp.dot(acat_ref[...], w_ref[...], preferred_element_type=F32)
    o_ref[...] = _layer_norm(alpha * x_ref[...] + g_ref[...] * y, lng_ref[...], lnb_ref[...])


def _proj_ln(a, w_o, j_na, x, ada, layer, row_fn, ln_g, ln_b, alpha, tm):
    m, d = x.shape
    return pl.pallas_call(
        functools.partial(_proj_ln_kernel, alpha=alpha),
        grid=(m // tm,),
        in_specs=[
            pl.BlockSpec((N_HEADS, tm, HEAD_DIM), lambda i: (0, i, 0)),
            pl.BlockSpec((None, d, d), lambda i: (j_na, 0, 0)),
            pl.BlockSpec((tm, d), lambda i: (i, 0)),
            _ada_spec(layer, 2, row_fn, d),
            _vec_spec(layer, d),
            _vec_spec(layer, d),
        ],
        out_specs=pl.BlockSpec((tm, d), lambda i: (i, 0)),
        out_shape=jax.ShapeDtypeStruct((m, d), F32),
        scratch_shapes=[pltpu.VMEM((tm, d), BF16)],
        compiler_params=_cparams("parallel"),
        name="out_proj_norm",
    )(a, w_o, x, ada, ln_g, ln_b)


def _pool_kernel(x_ref, xp_ref, xn_ref, sh_ref, sc_ref, g_ref, w_ref, ps_ref, lng_ref, lnb_ref, o_ref, ext_ref,
                 *, alpha, seq_len):
    tm, d = x_ref.shape
    gw = d // len(POOL_WINDOWS)
    start = pl.program_id(0) * tm
    pos0 = start % seq_len
    sc1 = 1.0 + sc_ref[...]
    sh = sh_ref[...]
    h = x_ref[...] * sc1 + sh
    ext_ref[0:POOL_HALO, :] = jnp.where(pos0 > 0, xp_ref[...] * sc1 + sh, 0.0)
    ext_ref[POOL_HALO:POOL_HALO + tm, :] = h
    ext_ref[POOL_HALO + tm:, :] = jnp.where(pos0 + tm < seq_len, xn_ref[...] * sc1 + sh, 0.0)
    n_ext = tm + 2 * POOL_HALO
    pos = pos0 + lax.broadcasted_iota(jnp.int32, (tm, 128), 0)
    ys = []
    for gi, w in enumerate(POOL_WINDOWS):
        cols = slice(gi * gw, (gi + 1) * gw)
        ext = ext_ref[:, cols]
        s = ext + pltpu.roll(ext, 1, axis=0)
        half = 1
        while 2 * half < w:
            s = pltpu.roll(s, half, axis=0) + pltpu.roll(s, n_ext - half, axis=0)
            half *= 2
        tot = s[POOL_HALO:POOL_HALO + tm]
        cnt = jnp.minimum(pos - w // 2 + w, seq_len) - jnp.maximum(pos - w // 2, 0)
        inv_cnt = jnp.tile(1.0 / cnt.astype(F32), (1, gw // 128))
        pooled = tot * inv_cnt - h[:, cols]
        ys.append(jnp.dot(pooled.astype(BF16), w_ref[gi], preferred_element_type=F32))
    y = jnp.concatenate(ys, axis=1) * ps_ref[...]
    o_ref[...] = _layer_norm(alpha * x_ref[...] + g_ref[...] * y, lng_ref[...], lnb_ref[...])


def _pool_mix(x, ada, layer, row_fn, w_pool, pool_scale, j_pool, ln_g, ln_b, alpha, tm, seq_len):
    m, d = x.shape
    hb = tm // POOL_HALO
    n_halo_blocks = m // POOL_HALO
    gw = d // len(POOL_WINDOWS)
    return pl.pallas_call(
        functools.partial(_pool_kernel, alpha=alpha, seq_len=seq_len),
        grid=(m // tm,),
        in_specs=[
            pl.BlockSpec((tm, d), lambda i: (i, 0)),
            pl.BlockSpec((POOL_HALO, d), lambda i: (jnp.maximum(i * hb - 1, 0), 0)),
            pl.BlockSpec((POOL_HALO, d), lambda i: (jnp.minimum((i + 1) * hb, n_halo_blocks - 1), 0)),
            _ada_spec(layer, 0, row_fn, d),
            _ada_spec(layer, 1, row_fn, d),
            _ada_spec(layer, 2, row_fn, d),
            pl.BlockSpec((None, len(POOL_WINDOWS), gw, gw), lambda i: (j_pool, 0, 0, 0)),
            _vec_spec(j_pool, d),
            _vec_spec(layer, d),
            _vec_spec(layer, d),
        ],
        out_specs=pl.BlockSpec((tm, d), lambda i: (i, 0)),
        out_shape=jax.ShapeDtypeStruct((m, d), F32),
        scratch_shapes=[pltpu.VMEM((tm + 2 * POOL_HALO, d), F32)],
        compiler_params=_cparams("parallel"),
        name="pool_mix_norm",
    )(x, x, x, ada, ada, ada, w_pool, pool_scale, ln_g, ln_b)


def _ffn_kernel(x_ref, sh_ref, sc_ref, g_ref, wg_ref, wu_ref, wo_ref, lng_ref, lnb_ref, *rest, alpha):
    n_cast = (len(rest) - 2) // 2
    cast_src, o_ref, cast_dst, h_ref = rest[:n_cast], rest[n_cast], rest[n_cast + 1:-1], rest[-1]
    f = pl.program_id(1)

    @pl.when(f == 0)
    def _():
        _modulate_rows(x_ref, sh_ref, sc_ref, h_ref)
        o_ref[...] = jnp.zeros_like(o_ref)

    _run_cast_side_job(cast_src, cast_dst)
    h = h_ref[...]
    gate = jnp.dot(h, wg_ref[...], preferred_element_type=F32)
    up = jnp.dot(h, wu_ref[...], preferred_element_type=F32)
    o_ref[...] += jnp.dot((_silu(gate) * up).astype(BF16), wo_ref[...], preferred_element_type=F32)

    @pl.when(f == pl.num_programs(1) - 1)
    def _():
        g, lng, lnb = g_ref[...], lng_ref[...], lnb_ref[...]

        def chunk(rows):
            o_ref[rows, :] = _layer_norm(alpha * x_ref[rows, :] + g * o_ref[rows, :], lng, lnb)
        _for_row_chunks(o_ref.shape[0], chunk)


def _ffn(x, ada, layer, row_fn, w_in, w_out, ln_g, ln_b, alpha, tm, tf=512, cast=None):
    m, d = x.shape
    d_ff = w_out.shape[0]
    nf = d_ff // tf
    cast_in, cast_out, cast_shape = _cast_side_job(*cast, m // tm, nf) if cast else ([], [], [])
    return pl.pallas_call(
        functools.partial(_ffn_kernel, alpha=alpha),
        grid=(m // tm, nf),
        in_specs=[
            pl.BlockSpec((tm, d), lambda i, f: (i, 0)),
            _ada_spec(layer, 3, row_fn, d),
            _ada_spec(layer, 4, row_fn, d),
            _ada_spec(layer, 5, row_fn, d),
            pl.BlockSpec((d, tf), lambda i, f: (0, f)),
            pl.BlockSpec((d, tf), lambda i, f: (0, nf + f)),
            pl.BlockSpec((tf, d), lambda i, f: (f, 0)),
            _vec_spec(layer, d),
            _vec_spec(layer, d),
        ] + cast_in,
        out_specs=[pl.BlockSpec((tm, d), lambda i, f: (i, 0))] + cast_out,
        out_shape=[jax.ShapeDtypeStruct((m, d), F32)] + cast_shape,
        scratch_shapes=[pltpu.VMEM((tm, d), BF16)],
        compiler_params=_cparams("parallel", "arbitrary"),
        name="ffn_norm",
    )(x, ada, ada, ada, w_in, w_in, w_out, ln_g, ln_b, *(cast[:2] if cast else ()))


def kernel(x, c, ctx, c_ctx, ada_w, ada_b, ln_mix_g, ln_mix_b, ln_ffn_g, ln_ffn_b,
           na_w_qkv, na_w_o, na_rpb, pool_w, pool_scale, ffn_w_in, ffn_w_out):
    batch, seq, d = x.shape
    n_ctx = ctx.shape[1]
    depth = ada_w.shape[0]
    n_mixers = 2
    assert seq % Q_TOK == 0 and seq == GRID_W * GRID_W and batch < COND_ROWS
    alpha = (2 * depth) ** 0.25
    last_na = max(i for i in range(depth) if i % n_mixers == 0)

    cond = jnp.concatenate([c, c_ctx[None], jnp.zeros((COND_ROWS - batch - 1, d), F32)], axis=0)
    ada = _ada_params(cond, ada_w, ada_b).reshape(depth, COND_ROWS, N_ADA, 1, d)

    tm = 512
    tm_big = 1024
    lat_row = lambda i: i // (seq // tm)
    lat_row_big = lambda i: i // (seq // tm_big)
    ctx_row = lambda i: batch
    ctx_tm = min(tm, batch * n_ctx)

    qscale = jnp.concatenate([jnp.full((d,), HEAD_DIM ** -0.5 * LOG2E, F32), jnp.ones((2 * d,), F32)])
    w_qkv = (na_w_qkv * qscale).astype(BF16)
    w_o = na_w_o.astype(BF16)
    w_pool = pool_w.astype(BF16)
    w_in = [None] * depth
    w_out = [None] * depth
    lmg, lmb = ln_mix_g.reshape(depth, 1, d), ln_mix_b.reshape(depth, 1, d)
    lfg, lfb = ln_ffn_g.reshape(depth, 1, d), ln_ffn_b.reshape(depth, 1, d)
    ps = pool_scale.reshape(-1, 1, d)
    bias_tiles = _bias_tiles(na_rpb.reshape((-1,) + na_rpb.shape[2:]))

    xl = x.reshape(batch * seq, d)
    xc = ctx.reshape(batch * n_ctx, d)
    for i in range(depth):
        use_na = i % n_mixers == 0
        j = i // n_mixers
        update_ctx = i < last_na
        if use_na:
            cast = (ffn_w_in, ffn_w_out, i) if w_in[i] is None else None
            q, k, vt, *converted = _qkv_proj(xl, ada, i, lat_row_big, w_qkv, j, tm_big, cast=cast)
            if converted:
                w_in[i], w_out[i] = converted
            q_c, k_c, vt_c = _qkv_proj(xc, ada, i, ctx_row, w_qkv, j, batch * n_ctx)
            att = _neighbourhood_attention(q, k, vt, k_c, vt_c, bias_tiles, j, batch)
            xl = _proj_ln(att, w_o, j, xl, ada, i, lat_row, lmg, lmb, alpha, tm)
            if update_ctx:
                att_c = _context_attention(q_c, k_c, vt_c, batch)
                xc = _proj_ln(att_c, w_o, j, xc, ada, i, ctx_row, lmg, lmb, alpha, ctx_tm)
        else:
            xl = _pool_mix(xl, ada, i, lat_row, w_pool, ps, j, lmg, lmb, alpha, tm, seq)
            if update_ctx:
                xc = _pool_mix(xc, ada, i, ctx_row, w_pool, ps, j, lmg, lmb, alpha, n_ctx, n_ctx)
        if w_in[i] is None:
            w_in[i], w_out[i] = ffn_w_in[i].astype(BF16), ffn_w_out[i].astype(BF16)
        cast = (ffn_w_in, ffn_w_out, i + 1) if i + 1 < depth else None
        xl, *converted = _ffn(xl, ada, i, lat_row_big, w_in[i], w_out[i], lfg, lfb, alpha, tm_big, cast=cast)
        if converted:
            w_in[i + 1], w_out[i + 1] = converted
        if update_ctx:
            xc, = _ffn(xc, ada, i, ctx_row, w_in[i], w_out[i], lfg, lfb, alpha, min(tm_big, batch * n_ctx))
    return xl.reshape(batch, seq, d)
```

```python
import functools
import math

import numpy as np
import jax
import jax.numpy as jnp
from jax import lax
from jax.experimental import pallas as pl
from jax.experimental.pallas import tpu as pltpu

F32 = jnp.float32
BF16 = jnp.bfloat16

GRID_W = 64
N_HEADS = 16
HEAD_DIM = 128
WIN_ROWS = 8
WIN_COLS = 16
POOL_WINDOWS = (2, 4, 8, 16)
N_ADA = 6
LN_EPS = 1e-5
LOG2E = math.log2(math.e)

Q_ROWS = 4
K_ROWS = Q_ROWS + WIN_ROWS
Q_TOK = Q_ROWS * GRID_W
K_TOK = K_ROWS * GRID_W
N_BIAS_TILES = 18
NA_OUT_CHUNK = 1024
POOL_HALO = 8
COND_ROWS = 8
ROW_CHUNK = 128
QKV_HEADS_PER_TILE = 4

VMEM_LIMIT = 60 * 1024 * 1024


def _cparams(*sem):
    return pltpu.CompilerParams(dimension_semantics=sem, vmem_limit_bytes=VMEM_LIMIT)


def _layer_norm(v, g, b):
    mu = jnp.mean(v, axis=-1, keepdims=True)
    d = v - mu
    var = jnp.mean(d * d, axis=-1, keepdims=True)
    return d * lax.rsqrt(var + LN_EPS) * g + b


def _silu(v):
    return v * jax.nn.sigmoid(v)


def _for_row_chunks(n_rows, fn):
    def body(c, carry):
        fn(pl.ds(pl.multiple_of(c * ROW_CHUNK, ROW_CHUNK), ROW_CHUNK))
        return carry
    lax.fori_loop(0, n_rows // ROW_CHUNK, body, 0)


def _modulate_rows(x_ref, sh_ref, sc_ref, h_ref):
    sc1 = 1.0 + sc_ref[...]
    sh = sh_ref[...]

    def chunk(rows):
        h_ref[rows, :] = (x_ref[rows, :] * sc1 + sh).astype(BF16)
    _for_row_chunks(x_ref.shape[0], chunk)


def _cast_side_job(w_in, w_out, layer, n_i, n_j):
    _, d, f2 = w_in.shape
    f = w_out.shape[1]
    bi, bj, ci = d // n_i, f2 // n_j, f // n_j
    in_specs = [pl.BlockSpec((None, bi, bj), lambda i, j: (layer, i, j)),
                pl.BlockSpec((None, ci, bi), lambda i, j: (layer, j, i))]
    out_specs = [pl.BlockSpec((bi, bj), lambda i, j: (i, j)), pl.BlockSpec((ci, bi), lambda i, j: (j, i))]
    out_shape = [jax.ShapeDtypeStruct((d, f2), BF16), jax.ShapeDtypeStruct((f, d), BF16)]
    return in_specs, out_specs, out_shape


def _run_cast_side_job(src_refs, dst_refs):
    for src, dst in zip(src_refs, dst_refs):
        dst[...] = src[...].astype(BF16)


def _ada_kernel(cond_ref, w_ref, b_ref, o_ref):
    s = _silu(cond_ref[...]).astype(BF16)
    o_ref[...] = jnp.dot(s, w_ref[...].astype(BF16), preferred_element_type=F32) + b_ref[...]


def _ada_params(cond, ada_w, ada_b, tn=1024):
    depth, d, n = ada_w.shape
    return pl.pallas_call(
        _ada_kernel,
        grid=(depth, n // tn),
        in_specs=[
            pl.BlockSpec((COND_ROWS, d), lambda l, j: (0, 0)),
            pl.BlockSpec((None, d, tn), lambda l, j: (l, 0, j)),
            pl.BlockSpec((None, 1, tn), lambda l, j: (l, 0, j)),
        ],
        out_specs=pl.BlockSpec((None, COND_ROWS, tn), lambda l, j: (l, 0, j)),
        out_shape=jax.ShapeDtypeStruct((depth, COND_ROWS, n), F32),
        compiler_params=_cparams("parallel", "parallel"),
        name="ada_params",
    )(cond, ada_w, ada_b.reshape(depth, 1, n))


def _ada_spec(layer, which, row_fn, d):
    return pl.BlockSpec((None, None, None, 1, d), lambda i, *_: (layer, row_fn(i), which, 0, 0))


def _vec_spec(layer, d):
    return pl.BlockSpec((None, 1, d), lambda i, *_: (layer, 0, 0))


def _qkv_kernel(x_ref, sh_ref, sc_ref, wq_ref, wk_ref, wv_ref, *rest):
    n_cast = (len(rest) - 4) // 2
    cast_src, (q_ref, k_ref, vt_ref), cast_dst, h_ref = (
        rest[:n_cast], rest[n_cast:n_cast + 3], rest[n_cast + 3:-1], rest[-1])

    @pl.when(pl.program_id(1) == 0)
    def _():
        _modulate_rows(x_ref, sh_ref, sc_ref, h_ref)

    _run_cast_side_job(cast_src, cast_dst)
    h = h_ref[...]
    for w_ref, o_ref, transposed in ((wq_ref, q_ref, False), (wk_ref, k_ref, False), (wv_ref, vt_ref, True)):
        res = jnp.dot(h, w_ref[...], preferred_element_type=F32)
        for hh in range(o_ref.shape[0]):
            part = res[:, hh * HEAD_DIM:(hh + 1) * HEAD_DIM]
            o_ref[hh] = (part.T if transposed else part).astype(BF16)


def _qkv_proj(x, ada, layer, row_fn, w_qkv, j_na, tm, cast=None):
    m, d = x.shape
    hpt = QKV_HEADS_PER_TILE
    tn = hpt * HEAD_DIM
    n_groups = N_HEADS // hpt
    cast_in, cast_out, cast_shape = _cast_side_job(*cast, m // tm, n_groups) if cast else ([], [], [])
    return pl.pallas_call(
        _qkv_kernel,
        grid=(m // tm, n_groups),
        in_specs=[
            pl.BlockSpec((tm, d), lambda i, j: (i, 0)),
            _ada_spec(layer, 0, row_fn, d),
            _ada_spec(layer, 1, row_fn, d),
            pl.BlockSpec((None, d, tn), lambda i, j: (j_na, 0, j)),
            pl.BlockSpec((None, d, tn), lambda i, j: (j_na, 0, n_groups + j)),
            pl.BlockSpec((None, d, tn), lambda i, j: (j_na, 0, 2 * n_groups + j)),
        ] + cast_in,
        out_specs=[
            pl.BlockSpec((hpt, tm, HEAD_DIM), lambda i, j: (j, i, 0)),
            pl.BlockSpec((hpt, tm, HEAD_DIM), lambda i, j: (j, i, 0)),
            pl.BlockSpec((hpt, HEAD_DIM, tm), lambda i, j: (j, 0, i)),
        ] + cast_out,
        out_shape=[
            jax.ShapeDtypeStruct((N_HEADS, m, HEAD_DIM), BF16),
            jax.ShapeDtypeStruct((N_HEADS, m, HEAD_DIM), BF16),
            jax.ShapeDtypeStruct((N_HEADS, HEAD_DIM, m), BF16),
        ] + cast_shape,
        scratch_shapes=[pltpu.VMEM((tm, d), BF16)],
        compiler_params=_cparams("parallel", "arbitrary"),
        name="qkv_proj",
    )(x, ada, ada, w_qkv, w_qkv, w_qkv, *(cast[:2] if cast else ()))


def _tile_offset(e):
    return np.where(e == 16, -4, np.where(e == 17, 4, e - (WIN_ROWS - 1)))


def _bias_rows(rpb):
    n_dr, n_dc = rpb.shape[1], rpb.shape[2]
    rev = rpb[..., ::-1] * LOG2E
    d = _tile_offset(np.arange(N_BIAS_TILES))
    first = rev[:, np.clip(d + WIN_ROWS - 1, 0, n_dr - 1)]
    second = rev[:, np.clip(d + WIN_ROWS - 2, 0, n_dr - 1)]
    gap = jnp.zeros(first.shape[:2] + (GRID_W - n_dc,), rpb.dtype)
    return jnp.concatenate([first[..., WIN_COLS - 1:], gap, second, gap, first[..., :WIN_COLS - 1]], axis=-1)


def _bias_mask():
    e = np.arange(N_BIAS_TILES)[:, None, None]
    ck = np.arange(GRID_W)[None, :, None]
    lane = np.arange(2 * GRID_W)[None, None, :]
    half, cq = lane // GRID_W, lane % GRID_W
    dr = _tile_offset(e) - half
    row_ok = (np.abs(dr) <= WIN_ROWS - 1) & ((e != 16) | (half == 0)) & ((e != 17) | (half == 1))
    ws = np.clip(cq - WIN_COLS // 2, 0, GRID_W - WIN_COLS)
    return (row_ok & (ck >= ws) & (ck < ws + WIN_COLS)).astype(np.float32)


def _tile_entry(d):
    return d + WIN_ROWS - 1


_TILES_TOP = [[(j, _tile_entry(j - 2 * u)) for j in range(0, WIN_ROWS)] for u in range(Q_ROWS // 2)]
_TILES_BOT = [[(j, _tile_entry(j - WIN_ROWS - 2 * u)) for j in range(K_ROWS - WIN_ROWS, K_ROWS)]
              for u in range(Q_ROWS // 2)]
_TILES_MID = [[(j, 16 if j - 2 * u - 4 == -4 else 17 if j - 2 * u - 4 == 4 else _tile_entry(j - 2 * u - 4))
               for j in range(2 * u, 2 * u + WIN_ROWS + 1)] for u in range(Q_ROWS // 2)]


def _na_kernel(q_ref, k_ref, vt_ref, kc_ref, vct_ref, brow_ref, mask_ref, o_ref,
               tab_ref, st_ref, pt_ref, sct_ref, pct_ref, otw_ref, linv_ref, *, n_blocks, n_ctx):
    lanes = 2 * GRID_W
    n_ctx_tiles = n_ctx // GRID_W
    n_pairs = Q_ROWS // 2
    last = n_blocks - 1
    zero_tile = jnp.zeros((GRID_W, lanes), BF16)
    nt = (((1,), (1,)), ((), ()))

    def tok(b):
        return b * Q_TOK if isinstance(b, int) else pl.multiple_of(b * Q_TOK, Q_TOK)

    def key_block(b):
        hi = n_blocks - K_ROWS // Q_ROWS
        return min(max(b - 1, 0), hi) if isinstance(b, int) else jnp.clip(b - 1, 0, hi)

    def zero_unused(pslot, tiles):
        for u in range(n_pairs):
            used = dict(tiles[u])
            for j in range(K_ROWS):
                if j not in used:
                    pt_ref[pslot, j * GRID_W:(j + 1) * GRID_W, u * lanes:(u + 1) * lanes] = zero_tile

    def scores(b, sslot):
        q = q_ref[pl.ds(tok(b), Q_TOK), :]
        st_ref[sslot] = lax.dot_general(k_ref[pl.ds(tok(key_block(b)), K_TOK), :], q, nt,
                                        preferred_element_type=F32)

    def softmax(b, sslot, pslot, tiles):
        for u in range(n_pairs):
            cols = slice(u * lanes, (u + 1) * lanes)
            qcols = pl.ds(tok(b) + u * lanes, lanes)

            def tile_scores(i):
                if i < len(tiles[u]):
                    j, e = tiles[u][i]
                    return st_ref[sslot, j * GRID_W:(j + 1) * GRID_W, cols] + tab_ref[e]
                t = i - len(tiles[u])
                return sct_ref[t * GRID_W:(t + 1) * GRID_W, qcols]

            n_tiles = len(tiles[u]) + n_ctx_tiles
            m = functools.reduce(jnp.maximum, [tile_scores(i) for i in range(n_tiles)])
            m = jnp.max(m, axis=0, keepdims=True)
            l = None
            for i in range(n_tiles):
                p = jnp.exp2(tile_scores(i) - m)
                l = p if l is None else l + p
                if i < len(tiles[u]):
                    j = tiles[u][i][0]
                    pt_ref[pslot, j * GRID_W:(j + 1) * GRID_W, cols] = p.astype(BF16)
                else:
                    t = i - len(tiles[u])
                    pct_ref[t * GRID_W:(t + 1) * GRID_W, qcols] = p.astype(BF16)
            linv_ref[0:1, qcols] = 1.0 / jnp.sum(l, axis=0, keepdims=True)

    def values(b, pslot):
        otw_ref[:, pl.ds(tok(b), Q_TOK)] = jnp.dot(vt_ref[:, pl.ds(tok(key_block(b)), K_TOK)], pt_ref[pslot],
                                                   preferred_element_type=F32)

    @pl.when(pl.program_id(1) == 0)
    def _():
        for e in range(N_BIAS_TILES):
            row = jnp.broadcast_to(brow_ref[e:e + 1, :], (GRID_W, lanes))
            toeplitz = pltpu.roll(row, 0, axis=1, stride=1, stride_axis=0)
            tab_ref[e] = jnp.where(mask_ref[e] != 0, toeplitz, -jnp.inf)

    zero_unused(0, _TILES_MID)
    zero_unused(1, _TILES_MID)
    zero_unused(2, _TILES_TOP)
    zero_unused(3, _TILES_BOT)
    sct_ref[...] = lax.dot_general(kc_ref[...], q_ref[...], nt, preferred_element_type=F32)

    scores(0, 0)
    scores(1, 1)
    softmax(0, 0, 2, _TILES_TOP)
    scores(2, 0)
    softmax(1, 1, 1, _TILES_MID)
    values(0, 2)

    finished = [0]

    def finish(n_done):
        while (finished[0] + 1) * NA_OUT_CHUNK <= n_done * Q_TOK:
            cols = slice(finished[0] * NA_OUT_CHUNK, (finished[0] + 1) * NA_OUT_CHUNK)
            ot = otw_ref[:, cols] + jnp.dot(vct_ref[...], pct_ref[:, cols], preferred_element_type=F32)
            o_ref[cols, :] = (ot * linv_ref[0:1, cols]).T.astype(BF16)
            finished[0] += 1

    for b in range(2, last - 1, 2):
        scores(b + 1, 1)
        softmax(b, 0, 0, _TILES_MID)
        values(b - 1, 1)
        finish(b)
        scores(b + 2, 0)
        softmax(b + 1, 1, 1, _TILES_MID)
        values(b, 0)
        finish(b + 1)
    scores(last, 1)
    softmax(last - 1, 0, 0, _TILES_MID)
    values(last - 2, 1)
    finish(last - 1)
    softmax(last, 1, 3, _TILES_BOT)
    values(last - 1, 0)
    values(last, 3)
    finish(n_blocks)


def _neighbourhood_attention(q, k, vt, k_c, vt_c, bias_rows, j_na, batch):
    m = q.shape[1]
    seq = m // batch
    n_ctx = k_c.shape[1] // batch
    n_blocks = seq // Q_TOK
    assert n_blocks % 2 == 0 and n_blocks >= 4 and seq % NA_OUT_CHUNK == 0
    kern = functools.partial(_na_kernel, n_blocks=n_blocks, n_ctx=n_ctx)
    return pl.pallas_call(
        kern,
        grid=(N_HEADS, batch),
        in_specs=[
            pl.BlockSpec((None, seq, HEAD_DIM), lambda h, b: (h, b, 0)),
            pl.BlockSpec((None, seq, HEAD_DIM), lambda h, b: (h, b, 0)),
            pl.BlockSpec((None, HEAD_DIM, seq), lambda h, b: (h, 0, b)),
            pl.BlockSpec((None, n_ctx, HEAD_DIM), lambda h, b: (h, b, 0)),
            pl.BlockSpec((None, HEAD_DIM, n_ctx), lambda h, b: (h, 0, b)),
            pl.BlockSpec((None, N_BIAS_TILES, 2 * GRID_W), lambda h, b: (j_na * N_HEADS + h, 0, 0)),
            pl.BlockSpec((N_BIAS_TILES, GRID_W, 2 * GRID_W), lambda h, b: (0, 0, 0)),
        ],
        out_specs=pl.BlockSpec((None, seq, HEAD_DIM), lambda h, b: (h, b, 0)),
        out_shape=jax.ShapeDtypeStruct((N_HEADS, m, HEAD_DIM), BF16),
        scratch_shapes=[
            pltpu.VMEM((N_BIAS_TILES, GRID_W, 2 * GRID_W), F32),
            pltpu.VMEM((2, K_TOK, Q_TOK), F32),
            pltpu.VMEM((4, K_TOK, Q_TOK), BF16),
            pltpu.VMEM((n_ctx, seq), F32),
            pltpu.VMEM((n_ctx, seq), BF16),
            pltpu.VMEM((HEAD_DIM, seq), F32),
            pltpu.VMEM((8, seq), F32),
        ],
        compiler_params=_cparams("parallel", "arbitrary"),
        name="neighbourhood_attention",
    )(q, k, vt, k_c, vt_c, bias_rows, jnp.asarray(_bias_mask()))


def _ctx_attn_kernel(q_ref, k_ref, vt_ref, o_ref):
    nt = (((1,), (1,)), ((), ()))
    for h in range(q_ref.shape[0]):
        st = lax.dot_general(k_ref[h], q_ref[h], nt, preferred_element_type=F32)
        p = jnp.exp2(st - jnp.max(st, axis=0, keepdims=True))
        inv_l = 1.0 / jnp.sum(p, axis=0, keepdims=True)
        ot = jnp.dot(vt_ref[h], p.astype(BF16), preferred_element_type=F32) * inv_l
        o_ref[h] = ot.T.astype(BF16)


def _context_attention(q_c, k_c, vt_c, batch):
    m = q_c.shape[1]
    n_ctx = m // batch
    return pl.pallas_call(
        _ctx_attn_kernel,
        grid=(batch,),
        in_specs=[
            pl.BlockSpec((N_HEADS, n_ctx, HEAD_DIM), lambda b: (0, b, 0)),
            pl.BlockSpec((N_HEADS, n_ctx, HEAD_DIM), lambda b: (0, b, 0)),
            pl.BlockSpec((N_HEADS, HEAD_DIM, n_ctx), lambda b: (0, 0, b)),
        ],
        out_specs=pl.BlockSpec((N_HEADS, n_ctx, HEAD_DIM), lambda b: (0, b, 0)),
        out_shape=jax.ShapeDtypeStruct((N_HEADS, m, HEAD_DIM), BF16),
        compiler_params=_cparams("parallel"),
        name="context_attention",
    )(q_c, k_c, vt_c)


def _proj_ln_kernel(a_ref, w_ref, x_ref, g_ref, lng_ref, lnb_ref, o_ref, acat_ref, *, alpha):
    for h in range(N_HEADS):
        acat_ref[:, h * HEAD_DIM:(h + 1) * HEAD_DIM] = a_ref[h]
    y = jnp.dot(acat_ref[...], w_ref[...], preferred_element_type=F32)
    o_ref[...] = _layer_norm(alpha * x_ref[...] + g_ref[...] * y, lng_ref[...], lnb_ref[...])


def _proj_ln(a, w_o, j_na, x, ada, layer, row_fn, ln_g, ln_b, alpha, tm):
    m, d = x.shape
    return pl.pallas_call(
        functools.partial(_proj_ln_kernel, alpha=alpha),
        grid=(m // tm,),
        in_specs=[
            pl.BlockSpec((N_HEADS, tm, HEAD_DIM), lambda i: (0, i, 0)),
            pl.BlockSpec((None, d, d), lambda i: (j_na, 0, 0)),
            pl.BlockSpec((tm, d), lambda i: (i, 0)),
            _ada_spec(layer, 2, row_fn, d),
            _vec_spec(layer, d),
            _vec_spec(layer, d),
        ],
        out_specs=pl.BlockSpec((tm, d), lambda i: (i, 0)),
        out_shape=jax.ShapeDtypeStruct((m, d), F32),
        scratch_shapes=[pltpu.VMEM((tm, d), BF16)],
        compiler_params=_cparams("parallel"),
        name="out_proj_norm",
    )(a, w_o, x, ada, ln_g, ln_b)


def _pool_kernel(x_ref, xp_ref, xn_ref, sh_ref, sc_ref, g_ref, w_ref, ps_ref, lng_ref, lnb_ref, o_ref, ext_ref,
                 *, alpha, seq_len):
    tm, d = x_ref.shape
    gw = d // len(POOL_WINDOWS)
    start = pl.program_id(0) * tm
    pos0 = start % seq_len
    sc1 = 1.0 + sc_ref[...]
    sh = sh_ref[...]
    h = x_ref[...] * sc1 + sh
    ext_ref[0:POOL_HALO, :] = jnp.where(pos0 > 0, xp_ref[...] * sc1 + sh, 0.0)
    ext_ref[POOL_HALO:POOL_HALO + tm, :] = h
    ext_ref[POOL_HALO + tm:, :] = jnp.where(pos0 + tm < seq_len, xn_ref[...] * sc1 + sh, 0.0)
    n_ext = tm + 2 * POOL_HALO
    pos = pos0 + lax.broadcasted_iota(jnp.int32, (tm, 128), 0)
    ys = []
    for gi, w in enumerate(POOL_WINDOWS):
        cols = slice(gi * gw, (gi + 1) * gw)
        ext = ext_ref[:, cols]
        s = ext + pltpu.roll(ext, 1, axis=0)
        half = 1
        while 2 * half < w:
            s = pltpu.roll(s, half, axis=0) + pltpu.roll(s, n_ext - half, axis=0)
            half *= 2
        tot = s[POOL_HALO:POOL_HALO + tm]
        cnt = jnp.minimum(pos - w // 2 + w, seq_len) - jnp.maximum(pos - w // 2, 0)
        inv_cnt = jnp.tile(1.0 / cnt.astype(F32), (1, gw // 128))
        pooled = tot * inv_cnt - h[:, cols]
        ys.append(jnp.dot(pooled.astype(BF16), w_ref[gi], preferred_element_type=F32))
    y = jnp.concatenate(ys, axis=1) * ps_ref[...]
    o_ref[...] = _layer_norm(alpha * x_ref[...] + g_ref[...] * y, lng_ref[...], lnb_ref[...])


def _pool_mix(x, ada, layer, row_fn, w_pool, pool_scale, j_pool, ln_g, ln_b, alpha, tm, seq_len):
    m, d = x.shape
    hb = tm // POOL_HALO
    n_halo_blocks = m // POOL_HALO
    gw = d // len(POOL_WINDOWS)
    return pl.pallas_call(
        functools.partial(_pool_kernel, alpha=alpha, seq_len=seq_len),
        grid=(m // tm,),
        in_specs=[
            pl.BlockSpec((tm, d), lambda i: (i, 0)),
            pl.BlockSpec((POOL_HALO, d), lambda i: (jnp.maximum(i * hb - 1, 0), 0)),
            pl.BlockSpec((POOL_HALO, d), lambda i: (jnp.minimum((i + 1) * hb, n_halo_blocks - 1), 0)),
            _ada_spec(layer, 0, row_fn, d),
            _ada_spec(layer, 1, row_fn, d),
            _ada_spec(layer, 2, row_fn, d),
            pl.BlockSpec((None, len(POOL_WINDOWS), gw, gw), lambda i: (j_pool, 0, 0, 0)),
            _vec_spec(j_pool, d),
            _vec_spec(layer, d),
            _vec_spec(layer, d),
        ],
        out_specs=pl.BlockSpec((tm, d), lambda i: (i, 0)),
        out_shape=jax.ShapeDtypeStruct((m, d), F32),
        scratch_shapes=[pltpu.VMEM((tm + 2 * POOL_HALO, d), F32)],
        compiler_params=_cparams("parallel"),
        name="pool_mix_norm",
    )(x, x, x, ada, ada, ada, w_pool, pool_scale, ln_g, ln_b)


def _ffn_kernel(x_ref, sh_ref, sc_ref, g_ref, wg_ref, wu_ref, wo_ref, lng_ref, lnb_ref, *rest, alpha):
    n_cast = (len(rest) - 2) // 2
    cast_src, o_ref, cast_dst, h_ref = rest[:n_cast], rest[n_cast], rest[n_cast + 1:-1], rest[-1]
    f = pl.program_id(1)

    @pl.when(f == 0)
    def _():
        _modulate_rows(x_ref, sh_ref, sc_ref, h_ref)
        o_ref[...] = jnp.zeros_like(o_ref)

    _run_cast_side_job(cast_src, cast_dst)
    h = h_ref[...]
    gate = jnp.dot(h, wg_ref[...], preferred_element_type=F32)
    up = jnp.dot(h, wu_ref[...], preferred_element_type=F32)
    o_ref[...] += jnp.dot((_silu(gate) * up).astype(BF16), wo_ref[...], preferred_element_type=F32)

    @pl.when(f == pl.num_programs(1) - 1)
    def _():
        g, lng, lnb = g_ref[...], lng_ref[...], lnb_ref[...]

        def chunk(rows):
            o_ref[rows, :] = _layer_norm(alpha * x_ref[rows, :] + g * o_ref[rows, :], lng, lnb)
        _for_row_chunks(o_ref.shape[0], chunk)


def _ffn(x, ada, layer, row_fn, w_in, w_out, ln_g, ln_b, alpha, tm, tf=512, cast=None):
    m, d = x.shape
    d_ff = w_out.shape[0]
    nf = d_ff // tf
    cast_in, cast_out, cast_shape = _cast_side_job(*cast, m // tm, nf) if cast else ([], [], [])
    return pl.pallas_call(
        functools.partial(_ffn_kernel, alpha=alpha),
        grid=(m // tm, nf),
        in_specs=[
            pl.BlockSpec((tm, d), lambda i, f: (i, 0)),
            _ada_spec(layer, 3, row_fn, d),
            _ada_spec(layer, 4, row_fn, d),
            _ada_spec(layer, 5, row_fn, d),
            pl.BlockSpec((d, tf), lambda i, f: (0, f)),
            pl.BlockSpec((d, tf), lambda i, f: (0, nf + f)),
            pl.BlockSpec((tf, d), lambda i, f: (f, 0)),
            _vec_spec(layer, d),
            _vec_spec(layer, d),
        ] + cast_in,
        out_specs=[pl.BlockSpec((tm, d), lambda i, f: (i, 0))] + cast_out,
        out_shape=[jax.ShapeDtypeStruct((m, d), F32)] + cast_shape,
        scratch_shapes=[pltpu.VMEM((tm, d), BF16)],
        compiler_params=_cparams("parallel", "arbitrary"),
        name="ffn_norm",
    )(x, ada, ada, ada, w_in, w_in, w_out, ln_g, ln_b, *(cast[:2] if cast else ()))


def kernel(x, c, ctx, c_ctx, ada_w, ada_b, ln_mix_g, ln_mix_b, ln_ffn_g, ln_ffn_b,
           na_w_qkv, na_w_o, na_rpb, pool_w, pool_scale, ffn_w_in, ffn_w_out):
    batch, seq, d = x.shape
    n_ctx = ctx.shape[1]
    depth = ada_w.shape[0]
    n_mixers = 2
    assert seq % Q_TOK == 0 and seq == GRID_W * GRID_W and batch < COND_ROWS
    alpha = (2 * depth) ** 0.25
    last_na = max(i for i in range(depth) if i % n_mixers == 0)

    cond = jnp.concatenate([c, c_ctx[None], jnp.zeros((COND_ROWS - batch - 1, d), F32)], axis=0)
    ada = _ada_params(cond, ada_w, ada_b).reshape(depth, COND_ROWS, N_ADA, 1, d)

    tm = 512
    tm_big = 1024
    lat_row = lambda i: i // (seq // tm)
    lat_row_big = lambda i: i // (seq // tm_big)
    ctx_row = lambda i: batch
    ctx_tm = min(tm, batch * n_ctx)

    qscale = jnp.concatenate([jnp.full((d,), HEAD_DIM ** -0.5 * LOG2E, F32), jnp.ones((2 * d,), F32)])
    w_qkv = (na_w_qkv * qscale).astype(BF16)
    w_o = na_w_o.astype(BF16)
    w_pool = pool_w.astype(BF16)
    w_in = [None] * depth
    w_out = [None] * depth
    lmg, lmb = ln_mix_g.reshape(depth, 1, d), ln_mix_b.reshape(depth, 1, d)
    lfg, lfb = ln_ffn_g.reshape(depth, 1, d), ln_ffn_b.reshape(depth, 1, d)
    ps = pool_scale.reshape(-1, 1, d)
    bias_rows = _bias_rows(na_rpb.reshape((-1,) + na_rpb.shape[2:]))

    xl = x.reshape(batch * seq, d)
    xc = ctx.reshape(batch * n_ctx, d)
    for i in range(depth):
        use_na = i % n_mixers == 0
        j = i // n_mixers
        update_ctx = i < last_na
        if use_na:
            cast = (ffn_w_in, ffn_w_out, i) if w_in[i] is None else None
            q, k, vt, *converted = _qkv_proj(xl, ada, i, lat_row_big, w_qkv, j, tm_big, cast=cast)
            if converted:
                w_in[i], w_out[i] = converted
            q_c, k_c, vt_c = _qkv_proj(xc, ada, i, ctx_row, w_qkv, j, batch * n_ctx)
            att = _neighbourhood_attention(q, k, vt, k_c, vt_c, bias_rows, j, batch)
            xl = _proj_ln(att, w_o, j, xl, ada, i, lat_row, lmg, lmb, alpha, tm)
            if update_ctx:
                att_c = _context_attention(q_c, k_c, vt_c, batch)
                xc = _proj_ln(att_c, w_o, j, xc, ada, i, ctx_row, lmg, lmb, alpha, ctx_tm)
        else:
            xl = _pool_mix(xl, ada, i, lat_row, w_pool, ps, j, lmg, lmb, alpha, tm, seq)
            if update_ctx:
                xc = _pool_mix(xc, ada, i, ctx_row, w_pool, ps, j, lmg, lmb, alpha, n_ctx, n_ctx)
        if w_in[i] is None:
            w_in[i], w_out[i] = ffn_w_in[i].astype(BF16), ffn_w_out[i].astype(BF16)
        cast = (ffn_w_in, ffn_w_out, i + 1) if i + 1 < depth else None
        xl, *converted = _ffn(xl, ada, i, lat_row_big, w_in[i], w_out[i], lfg, lfb, alpha, tm_big, cast=cast)
        if converted:
            w_in[i + 1], w_out[i + 1] = converted
        if update_ctx:
            xc, = _ffn(xc, ada, i, ctx_row, w_in[i], w_out[i], lfg, lfb, alpha, min(tm_big, batch * n_ctx))
    return xl.reshape(batch, seq, d)
```

```python
import functools
import math

import numpy as np
import jax
import jax.numpy as jnp
from jax import lax
from jax.experimental import pallas as pl
from jax.experimental.pallas import tpu as pltpu

F32 = jnp.float32
BF16 = jnp.bfloat16

GRID_W = 64
N_HEADS = 16
HEAD_DIM = 128
WIN_ROWS = 8
WIN_COLS = 16
POOL_WINDOWS = (2, 4, 8, 16)
N_ADA = 6
LN_EPS = 1e-5
LOG2E = math.log2(math.e)

Q_ROWS = 4
K_ROWS = Q_ROWS + WIN_ROWS
Q_TOK = Q_ROWS * GRID_W
K_TOK = K_ROWS * GRID_W
N_BIAS_TILES = 18
NA_OUT_CHUNK = 2048
POOL_HALO = 8
COND_ROWS = 8
ROW_CHUNK = 128
QKV_HEADS_PER_TILE = 4

VMEM_LIMIT = 60 * 1024 * 1024


def _cparams(*sem):
    return pltpu.CompilerParams(dimension_semantics=sem, vmem_limit_bytes=VMEM_LIMIT)


def _layer_norm(v, g, b):
    mu = jnp.mean(v, axis=-1, keepdims=True)
    d = v - mu
    var = jnp.mean(d * d, axis=-1, keepdims=True)
    return d * lax.rsqrt(var + LN_EPS) * g + b


def _silu(v):
    return v * jax.nn.sigmoid(v)


def _for_row_chunks(n_rows, fn):
    def body(c, carry):
        fn(pl.ds(pl.multiple_of(c * ROW_CHUNK, ROW_CHUNK), ROW_CHUNK))
        return carry
    lax.fori_loop(0, n_rows // ROW_CHUNK, body, 0)


def _modulate_rows(x_ref, sh_ref, sc_ref, h_ref):
    sc1 = 1.0 + sc_ref[...]
    sh = sh_ref[...]

    def chunk(rows):
        h_ref[rows, :] = (x_ref[rows, :] * sc1 + sh).astype(BF16)
    _for_row_chunks(x_ref.shape[0], chunk)


def _cast_side_job(w_in, w_out, layer, n_i, n_j):
    _, d, f2 = w_in.shape
    f = w_out.shape[1]
    bi, bj, ci = d // n_i, f2 // n_j, f // n_j
    in_specs = [pl.BlockSpec((None, bi, bj), lambda i, j: (layer, i, j)),
                pl.BlockSpec((None, ci, bi), lambda i, j: (layer, j, i))]
    out_specs = [pl.BlockSpec((bi, bj), lambda i, j: (i, j)), pl.BlockSpec((ci, bi), lambda i, j: (j, i))]
    out_shape = [jax.ShapeDtypeStruct((d, f2), BF16), jax.ShapeDtypeStruct((f, d), BF16)]
    return in_specs, out_specs, out_shape


def _run_cast_side_job(src_refs, dst_refs):
    for src, dst in zip(src_refs, dst_refs):
        dst[...] = src[...].astype(BF16)


def _ada_kernel(cond_ref, w_ref, b_ref, o_ref):
    s = _silu(cond_ref[...]).astype(BF16)
    o_ref[...] = jnp.dot(s, w_ref[...].astype(BF16), preferred_element_type=F32) + b_ref[...]


def _ada_params(cond, ada_w, ada_b, tn=1024):
    depth, d, n = ada_w.shape
    return pl.pallas_call(
        _ada_kernel,
        grid=(depth, n // tn),
        in_specs=[
            pl.BlockSpec((COND_ROWS, d), lambda l, j: (0, 0)),
            pl.BlockSpec((None, d, tn), lambda l, j: (l, 0, j)),
            pl.BlockSpec((None, 1, tn), lambda l, j: (l, 0, j)),
        ],
        out_specs=pl.BlockSpec((None, COND_ROWS, tn), lambda l, j: (l, 0, j)),
        out_shape=jax.ShapeDtypeStruct((depth, COND_ROWS, n), F32),
        compiler_params=_cparams("parallel", "parallel"),
        name="ada_params",
    )(cond, ada_w, ada_b.reshape(depth, 1, n))


def _ada_spec(layer, which, row_fn, d):
    return pl.BlockSpec((None, None, None, 1, d), lambda i, *_: (layer, row_fn(i), which, 0, 0))


def _vec_spec(layer, d):
    return pl.BlockSpec((None, 1, d), lambda i, *_: (layer, 0, 0))


def _qkv_kernel(x_ref, sh_ref, sc_ref, wq_ref, wk_ref, wv_ref, *rest):
    n_cast = (len(rest) - 4) // 2
    cast_src, (q_ref, k_ref, vt_ref), cast_dst, h_ref = (
        rest[:n_cast], rest[n_cast:n_cast + 3], rest[n_cast + 3:-1], rest[-1])

    @pl.when(pl.program_id(1) == 0)
    def _():
        _modulate_rows(x_ref, sh_ref, sc_ref, h_ref)

    _run_cast_side_job(cast_src, cast_dst)
    h = h_ref[...]
    for w_ref, o_ref, transposed in ((wq_ref, q_ref, False), (wk_ref, k_ref, False), (wv_ref, vt_ref, True)):
        res = jnp.dot(h, w_ref[...], preferred_element_type=F32)
        for hh in range(o_ref.shape[0]):
            part = res[:, hh * HEAD_DIM:(hh + 1) * HEAD_DIM]
            o_ref[hh] = (part.T if transposed else part).astype(BF16)


def _qkv_proj(x, ada, layer, row_fn, w_qkv, j_na, tm, cast=None):
    m, d = x.shape
    hpt = QKV_HEADS_PER_TILE
    tn = hpt * HEAD_DIM
    n_groups = N_HEADS // hpt
    cast_in, cast_out, cast_shape = _cast_side_job(*cast, m // tm, n_groups) if cast else ([], [], [])
    return pl.pallas_call(
        _qkv_kernel,
        grid=(m // tm, n_groups),
        in_specs=[
            pl.BlockSpec((tm, d), lambda i, j: (i, 0)),
            _ada_spec(layer, 0, row_fn, d),
            _ada_spec(layer, 1, row_fn, d),
            pl.BlockSpec((None, d, tn), lambda i, j: (j_na, 0, j)),
            pl.BlockSpec((None, d, tn), lambda i, j: (j_na, 0, n_groups + j)),
            pl.BlockSpec((None, d, tn), lambda i, j: (j_na, 0, 2 * n_groups + j)),
        ] + cast_in,
        out_specs=[
            pl.BlockSpec((hpt, tm, HEAD_DIM), lambda i, j: (j, i, 0)),
            pl.BlockSpec((hpt, tm, HEAD_DIM), lambda i, j: (j, i, 0)),
            pl.BlockSpec((hpt, HEAD_DIM, tm), lambda i, j: (j, 0, i)),
        ] + cast_out,
        out_shape=[
            jax.ShapeDtypeStruct((N_HEADS, m, HEAD_DIM), BF16),
            jax.ShapeDtypeStruct((N_HEADS, m, HEAD_DIM), BF16),
            jax.ShapeDtypeStruct((N_HEADS, HEAD_DIM, m), BF16),
        ] + cast_shape,
        scratch_shapes=[pltpu.VMEM((tm, d), BF16)],
        compiler_params=_cparams("parallel", "arbitrary"),
        name="qkv_proj",
    )(x, ada, ada, w_qkv, w_qkv, w_qkv, *(cast[:2] if cast else ()))


def _tile_offset(e):
    return np.where(e == 16, -4, np.where(e == 17, 4, e - (WIN_ROWS - 1)))


def _bias_rows(rpb):
    n_dr, n_dc = rpb.shape[1], rpb.shape[2]
    rev = rpb[..., ::-1] * LOG2E
    d = _tile_offset(np.arange(N_BIAS_TILES))
    first = rev[:, np.clip(d + WIN_ROWS - 1, 0, n_dr - 1)]
    second = rev[:, np.clip(d + WIN_ROWS - 2, 0, n_dr - 1)]
    gap = jnp.zeros(first.shape[:2] + (GRID_W - n_dc,), rpb.dtype)
    return jnp.concatenate([first[..., WIN_COLS - 1:], gap, second, gap, first[..., :WIN_COLS - 1]], axis=-1)


def _bias_mask():
    e = np.arange(N_BIAS_TILES)[:, None, None]
    ck = np.arange(GRID_W)[None, :, None]
    lane = np.arange(2 * GRID_W)[None, None, :]
    half, cq = lane // GRID_W, lane % GRID_W
    dr = _tile_offset(e) - half
    row_ok = (np.abs(dr) <= WIN_ROWS - 1) & ((e != 16) | (half == 0)) & ((e != 17) | (half == 1))
    ws = np.clip(cq - WIN_COLS // 2, 0, GRID_W - WIN_COLS)
    return (row_ok & (ck >= ws) & (ck < ws + WIN_COLS)).astype(np.float32)


def _tile_entry(d):
    return d + WIN_ROWS - 1


_TILES_TOP = [[(j, _tile_entry(j - 2 * u)) for j in range(0, WIN_ROWS)] for u in range(Q_ROWS // 2)]
_TILES_BOT = [[(j, _tile_entry(j - WIN_ROWS - 2 * u)) for j in range(K_ROWS - WIN_ROWS, K_ROWS)]
              for u in range(Q_ROWS // 2)]
_TILES_MID = [[(j, 16 if j - 2 * u - 4 == -4 else 17 if j - 2 * u - 4 == 4 else _tile_entry(j - 2 * u - 4))
               for j in range(2 * u, 2 * u + WIN_ROWS + 1)] for u in range(Q_ROWS // 2)]


def _na_kernel(q_ref, k_ref, vt_ref, kc_ref, vct_ref, brow_ref, mask_ref, o_ref,
               tab_ref, st_ref, pt_ref, sct_ref, pct_ref, otw_ref, linv_ref, *, n_blocks, n_ctx):
    lanes = 2 * GRID_W
    n_ctx_tiles = n_ctx // GRID_W
    n_pairs = Q_ROWS // 2
    last = n_blocks - 1
    zero_tile = jnp.zeros((GRID_W, lanes), BF16)
    nt = (((1,), (1,)), ((), ()))

    def tok(b):
        return b * Q_TOK if isinstance(b, int) else pl.multiple_of(b * Q_TOK, Q_TOK)

    def key_block(b):
        hi = n_blocks - K_ROWS // Q_ROWS
        return min(max(b - 1, 0), hi) if isinstance(b, int) else jnp.clip(b - 1, 0, hi)

    def zero_unused(pslot, tiles):
        for u in range(n_pairs):
            used = dict(tiles[u])
            for j in range(K_ROWS):
                if j not in used:
                    pt_ref[pslot, j * GRID_W:(j + 1) * GRID_W, u * lanes:(u + 1) * lanes] = zero_tile

    def scores(b, sslot):
        q = q_ref[pl.ds(tok(b), Q_TOK), :]
        st_ref[sslot] = lax.dot_general(k_ref[pl.ds(tok(key_block(b)), K_TOK), :], q, nt,
                                        preferred_element_type=F32)

    def softmax(b, sslot, pslot, tiles):
        for u in range(n_pairs):
            cols = slice(u * lanes, (u + 1) * lanes)
            qcols = pl.ds(tok(b) + u * lanes, lanes)

            def tile_scores(i):
                if i < len(tiles[u]):
                    j, e = tiles[u][i]
                    return st_ref[sslot, j * GRID_W:(j + 1) * GRID_W, cols] + tab_ref[e]
                t = i - len(tiles[u])
                return sct_ref[t * GRID_W:(t + 1) * GRID_W, qcols]

            n_tiles = len(tiles[u]) + n_ctx_tiles
            m = tile_scores(0)
            for i in range(1, n_tiles):
                m = jnp.maximum(m, tile_scores(i))
            m = jnp.max(m, axis=0, keepdims=True)
            l = None
            for i in range(n_tiles):
                p = jnp.exp2(tile_scores(i) - m)
                l = p if l is None else l + p
                if i < len(tiles[u]):
                    j = tiles[u][i][0]
                    pt_ref[pslot, j * GRID_W:(j + 1) * GRID_W, cols] = p.astype(BF16)
                else:
                    t = i - len(tiles[u])
                    pct_ref[t * GRID_W:(t + 1) * GRID_W, qcols] = p.astype(BF16)
            linv_ref[0:1, qcols] = 1.0 / jnp.sum(l, axis=0, keepdims=True)

    def values(b, pslot):
        otw_ref[:, pl.ds(tok(b), Q_TOK)] = jnp.dot(vt_ref[:, pl.ds(tok(key_block(b)), K_TOK)], pt_ref[pslot],
                                                   preferred_element_type=F32)

    @pl.when(pl.program_id(1) == 0)
    def _():
        for e in range(N_BIAS_TILES):
            row = jnp.broadcast_to(brow_ref[e:e + 1, :], (GRID_W, lanes))
            toeplitz = pltpu.roll(row, 0, axis=1, stride=1, stride_axis=0)
            tab_ref[e] = jnp.where(mask_ref[e] != 0, toeplitz, -jnp.inf)

    zero_unused(0, _TILES_MID)
    zero_unused(1, _TILES_MID)
    zero_unused(2, _TILES_TOP)
    zero_unused(3, _TILES_BOT)
    sct_ref[...] = lax.dot_general(kc_ref[...], q_ref[...], nt, preferred_element_type=F32)

    scores(0, 0)
    scores(1, 1)
    softmax(0, 0, 2, _TILES_TOP)
    scores(2, 0)
    softmax(1, 1, 1, _TILES_MID)
    values(0, 2)

    finished = [0]

    def finish(n_done):
        while (finished[0] + 1) * NA_OUT_CHUNK <= n_done * Q_TOK:
            cols = slice(finished[0] * NA_OUT_CHUNK, (finished[0] + 1) * NA_OUT_CHUNK)
            ot = otw_ref[:, cols] + jnp.dot(vct_ref[...], pct_ref[:, cols], preferred_element_type=F32)
            o_ref[cols, :] = (ot * linv_ref[0:1, cols]).T.astype(BF16)
            finished[0] += 1

    for b in range(2, last - 1, 2):
        scores(b + 1, 1)
        softmax(b, 0, 0, _TILES_MID)
        values(b - 1, 1)
        finish(b)
        scores(b + 2, 0)
        softmax(b + 1, 1, 1, _TILES_MID)
        values(b, 0)
        finish(b + 1)
    scores(last, 1)
    softmax(last - 1, 0, 0, _TILES_MID)
    values(last - 2, 1)
    finish(last - 1)
    softmax(last, 1, 3, _TILES_BOT)
    values(last - 1, 0)
    values(last, 3)
    finish(n_blocks)


def _neighbourhood_attention(q, k, vt, k_c, vt_c, bias_rows, j_na, batch):
    m = q.shape[1]
    seq = m // batch
    n_ctx = k_c.shape[1] // batch
    n_blocks = seq // Q_TOK
    assert n_blocks % 2 == 0 and n_blocks >= 4 and seq % NA_OUT_CHUNK == 0
    kern = functools.partial(_na_kernel, n_blocks=n_blocks, n_ctx=n_ctx)
    return pl.pallas_call(
        kern,
        grid=(N_HEADS, batch),
        in_specs=[
            pl.BlockSpec((None, seq, HEAD_DIM), lambda h, b: (h, b, 0)),
            pl.BlockSpec((None, seq, HEAD_DIM), lambda h, b: (h, b, 0)),
            pl.BlockSpec((None, HEAD_DIM, seq), lambda h, b: (h, 0, b)),
            pl.BlockSpec((None, n_ctx, HEAD_DIM), lambda h, b: (h, b, 0)),
            pl.BlockSpec((None, HEAD_DIM, n_ctx), lambda h, b: (h, 0, b)),
            pl.BlockSpec((None, N_BIAS_TILES, 2 * GRID_W), lambda h, b: (j_na * N_HEADS + h, 0, 0)),
            pl.BlockSpec((N_BIAS_TILES, GRID_W, 2 * GRID_W), lambda h, b: (0, 0, 0)),
        ],
        out_specs=pl.BlockSpec((None, seq, HEAD_DIM), lambda h, b: (h, b, 0)),
        out_shape=jax.ShapeDtypeStruct((N_HEADS, m, HEAD_DIM), BF16),
        scratch_shapes=[
            pltpu.VMEM((N_BIAS_TILES, GRID_W, 2 * GRID_W), F32),
            pltpu.VMEM((2, K_TOK, Q_TOK), F32),
            pltpu.VMEM((4, K_TOK, Q_TOK), BF16),
            pltpu.VMEM((n_ctx, seq), F32),
            pltpu.VMEM((n_ctx, seq), BF16),
            pltpu.VMEM((HEAD_DIM, seq), F32),
            pltpu.VMEM((8, seq), F32),
        ],
        compiler_params=_cparams("parallel", "arbitrary"),
        name="neighbourhood_attention",
    )(q, k, vt, k_c, vt_c, bias_rows, jnp.asarray(_bias_mask()))


def _ctx_attn_kernel(q_ref, k_ref, vt_ref, o_ref):
    nt = (((1,), (1,)), ((), ()))
    for h in range(q_ref.shape[0]):
        st = lax.dot_general(k_ref[h], q_ref[h], nt, preferred_element_type=F32)
        p = jnp.exp2(st - jnp.max(st, axis=0, keepdims=True))
        inv_l = 1.0 / jnp.sum(p, axis=0, keepdims=True)
        ot = jnp.dot(vt_ref[h], p.astype(BF16), preferred_element_type=F32) * inv_l
        o_ref[h] = ot.T.astype(BF16)


def _context_attention(q_c, k_c, vt_c, batch):
    m = q_c.shape[1]
    n_ctx = m // batch
    return pl.pallas_call(
        _ctx_attn_kernel,
        grid=(batch,),
        in_specs=[
            pl.BlockSpec((N_HEADS, n_ctx, HEAD_DIM), lambda b: (0, b, 0)),
            pl.BlockSpec((N_HEADS, n_ctx, HEAD_DIM), lambda b: (0, b, 0)),
            pl.BlockSpec((N_HEADS, HEAD_DIM, n_ctx), lambda b: (0, 0, b)),
        ],
        out_specs=pl.BlockSpec((N_HEADS, n_ctx, HEAD_DIM), lambda b: (0, b, 0)),
        out_shape=jax.ShapeDtypeStruct((N_HEADS, m, HEAD_DIM), BF16),
        compiler_params=_cparams("parallel"),
        name="context_attention",
    )(q_c, k_c, vt_c)


def _proj_ln_kernel(a_ref, w_ref, x_ref, g_ref, lng_ref, lnb_ref, o_ref, acat_ref, *, alpha):
    for h in range(N_HEADS):
        acat_ref[:, h * HEAD_DIM:(h + 1) * HEAD_DIM] = a_ref[h]
    y = jnp.dot(acat_ref[...], w_ref[...], preferred_element_type=F32)
    o_ref[...] = _layer_norm(alpha * x_ref[...] + g_ref[...] * y, lng_ref[...], lnb_ref[...])


def _proj_ln(a, w_o, j_na, x, ada, layer, row_fn, ln_g, ln_b, alpha, tm):
    m, d = x.shape
    return pl.pallas_call(
        functools.partial(_proj_ln_kernel, alpha=alpha),
        grid=(m // tm,),
        in_specs=[
            pl.BlockSpec((N_HEADS, tm, HEAD_DIM), lambda i: (0, i, 0)),
            pl.BlockSpec((None, d, d), lambda i: (j_na, 0, 0)),
            pl.BlockSpec((tm, d), lambda i: (i, 0)),
            _ada_spec(layer, 2, row_fn, d),
            _vec_spec(layer, d),
            _vec_spec(layer, d),
        ],
        out_specs=pl.BlockSpec((tm, d), lambda i: (i, 0)),
        out_shape=jax.ShapeDtypeStruct((m, d), F32),
        scratch_shapes=[pltpu.VMEM((tm, d), BF16)],
        compiler_params=_cparams("parallel"),
        name="out_proj_norm",
    )(a, w_o, x, ada, ln_g, ln_b)


def _pool_kernel(x_ref, xp_ref, xn_ref, sh_ref, sc_ref, g_ref, w_ref, ps_ref, lng_ref, lnb_ref, o_ref, ext_ref,
                 *, alpha, seq_len):
    tm, d = x_ref.shape
    gw = d // len(POOL_WINDOWS)
    start = pl.program_id(0) * tm
    pos0 = start % seq_len
    sc1 = 1.0 + sc_ref[...]
    sh = sh_ref[...]
    h = x_ref[...] * sc1 + sh
    ext_ref[0:POOL_HALO, :] = jnp.where(pos0 > 0, xp_ref[...] * sc1 + sh, 0.0)
    ext_ref[POOL_HALO:POOL_HALO + tm, :] = h
    ext_ref[POOL_HALO + tm:, :] = jnp.where(pos0 + tm < seq_len, xn_ref[...] * sc1 + sh, 0.0)
    n_ext = tm + 2 * POOL_HALO
    pos = pos0 + lax.broadcasted_iota(jnp.int32, (tm, 128), 0)
    ys = []
    for gi, w in enumerate(POOL_WINDOWS):
        cols = slice(gi * gw, (gi + 1) * gw)
        ext = ext_ref[:, cols]
        s = ext + pltpu.roll(ext, 1, axis=0)
        half = 1
        while 2 * half < w:
            s = pltpu.roll(s, half, axis=0) + pltpu.roll(s, n_ext - half, axis=0)
            half *= 2
        tot = s[POOL_HALO:POOL_HALO + tm]
        cnt = jnp.minimum(pos - w // 2 + w, seq_len) - jnp.maximum(pos - w // 2, 0)
        inv_cnt = jnp.tile(1.0 / cnt.astype(F32), (1, gw // 128))
        pooled = tot * inv_cnt - h[:, cols]
        ys.append(jnp.dot(pooled.astype(BF16), w_ref[gi], preferred_element_type=F32))
    y = jnp.concatenate(ys, axis=1) * ps_ref[...]
    o_ref[...] = _layer_norm(alpha * x_ref[...] + g_ref[...] * y, lng_ref[...], lnb_ref[...])


def _pool_mix(x, ada, layer, row_fn, w_pool, pool_scale, j_pool, ln_g, ln_b, alpha, tm, seq_len):
    m, d = x.shape
    hb = tm // POOL_HALO
    n_halo_blocks = m // POOL_HALO
    gw = d // len(POOL_WINDOWS)
    return pl.pallas_call(
        functools.partial(_pool_kernel, alpha=alpha, seq_len=seq_len),
        grid=(m // tm,),
        in_specs=[
            pl.BlockSpec((tm, d), lambda i: (i, 0)),
            pl.BlockSpec((POOL_HALO, d), lambda i: (jnp.maximum(i * hb - 1, 0), 0)),
            pl.BlockSpec((POOL_HALO, d), lambda i: (jnp.minimum((i + 1) * hb, n_halo_blocks - 1), 0)),
            _ada_spec(layer, 0, row_fn, d),
            _ada_spec(layer, 1, row_fn, d),
            _ada_spec(layer, 2, row_fn, d),
            pl.BlockSpec((None, len(POOL_WINDOWS), gw, gw), lambda i: (j_pool, 0, 0, 0)),
            _vec_spec(j_pool, d),
            _vec_spec(layer, d),
            _vec_spec(layer, d),
        ],
        out_specs=pl.BlockSpec((tm, d), lambda i: (i, 0)),
        out_shape=jax.ShapeDtypeStruct((m, d), F32),
        scratch_shapes=[pltpu.VMEM((tm + 2 * POOL_HALO, d), F32)],
        compiler_params=_cparams("parallel"),
        name="pool_mix_norm",
    )(x, x, x, ada, ada, ada, w_pool, pool_scale, ln_g, ln_b)


def _ffn_kernel(x_ref, sh_ref, sc_ref, g_ref, wg_ref, wu_ref, wo_ref, lng_ref, lnb_ref, *rest, alpha):
    n_cast = (len(rest) - 2) // 2
    cast_src, o_ref, cast_dst, h_ref = rest[:n_cast], rest[n_cast], rest[n_cast + 1:-1], rest[-1]
    f = pl.program_id(1)

    def partial_product():
        _run_cast_side_job(cast_src, cast_dst)
        h = h_ref[...]
        gate = jnp.dot(h, wg_ref[...], preferred_element_type=F32)
        up = jnp.dot(h, wu_ref[...], preferred_element_type=F32)
        return jnp.dot((_silu(gate) * up).astype(BF16), wo_ref[...], preferred_element_type=F32)

    @pl.when(f == 0)
    def _():
        _modulate_rows(x_ref, sh_ref, sc_ref, h_ref)
        o_ref[...] = partial_product()

    @pl.when(f > 0)
    def _():
        o_ref[...] += partial_product()

    @pl.when(f == pl.num_programs(1) - 1)
    def _():
        g, lng, lnb = g_ref[...], lng_ref[...], lnb_ref[...]

        def chunk(rows):
            o_ref[rows, :] = _layer_norm(alpha * x_ref[rows, :] + g * o_ref[rows, :], lng, lnb)
        _for_row_chunks(o_ref.shape[0], chunk)


def _ffn(x, ada, layer, row_fn, w_in, w_out, ln_g, ln_b, alpha, tm, tf=512, cast=None):
    m, d = x.shape
    d_ff = w_out.shape[0]
    nf = d_ff // tf
    cast_in, cast_out, cast_shape = _cast_side_job(*cast, m // tm, nf) if cast else ([], [], [])
    return pl.pallas_call(
        functools.partial(_ffn_kernel, alpha=alpha),
        grid=(m // tm, nf),
        in_specs=[
            pl.BlockSpec((tm, d), lambda i, f: (i, 0)),
            _ada_spec(layer, 3, row_fn, d),
            _ada_spec(layer, 4, row_fn, d),
            _ada_spec(layer, 5, row_fn, d),
            pl.BlockSpec((d, tf), lambda i, f: (0, f)),
            pl.BlockSpec((d, tf), lambda i, f: (0, nf + f)),
            pl.BlockSpec((tf, d), lambda i, f: (f, 0)),
            _vec_spec(layer, d),
            _vec_spec(layer, d),
        ] + cast_in,
        out_specs=[pl.BlockSpec((tm, d), lambda i, f: (i, 0))] + cast_out,
        out_shape=[jax.ShapeDtypeStruct((m, d), F32)] + cast_shape,
        scratch_shapes=[pltpu.VMEM((tm, d), BF16)],
        compiler_params=_cparams("parallel", "arbitrary"),
        name="ffn_norm",
    )(x, ada, ada, ada, w_in, w_in, w_out, ln_g, ln_b, *(cast[:2] if cast else ()))


def kernel(x, c, ctx, c_ctx, ada_w, ada_b, ln_mix_g, ln_mix_b, ln_ffn_g, ln_ffn_b,
           na_w_qkv, na_w_o, na_rpb, pool_w, pool_scale, ffn_w_in, ffn_w_out):
    batch, seq, d = x.shape
    n_ctx = ctx.shape[1]
    depth = ada_w.shape[0]
    n_mixers = 2
    assert seq % Q_TOK == 0 and seq == GRID_W * GRID_W and batch < COND_ROWS
    alpha = (2 * depth) ** 0.25
    last_na = max(i for i in range(depth) if i % n_mixers == 0)

    cond = jnp.concatenate([c, c_ctx[None], jnp.zeros((COND_ROWS - batch - 1, d), F32)], axis=0)
    ada = _ada_params(cond, ada_w, ada_b).reshape(depth, COND_ROWS, N_ADA, 1, d)

    tm = 512
    tm_big = 1024
    lat_row = lambda i: i // (seq // tm)
    lat_row_big = lambda i: i // (seq // tm_big)
    ctx_row = lambda i: batch
    ctx_tm = min(tm, batch * n_ctx)

    qscale = jnp.concatenate([jnp.full((d,), HEAD_DIM ** -0.5 * LOG2E, F32), jnp.ones((2 * d,), F32)])
    w_qkv = (na_w_qkv * qscale).astype(BF16)
    w_o = na_w_o.astype(BF16)
    w_pool = pool_w.astype(BF16)
    w_in = [None] * depth
    w_out = [None] * depth
    lmg, lmb = ln_mix_g.reshape(depth, 1, d), ln_mix_b.reshape(depth, 1, d)
    lfg, lfb = ln_ffn_g.reshape(depth, 1, d), ln_ffn_b.reshape(depth, 1, d)
    ps = pool_scale.reshape(-1, 1, d)
    bias_rows = _bias_rows(na_rpb.reshape((-1,) + na_rpb.shape[2:]))

    xl = x.reshape(batch * seq, d)
    xc = ctx.reshape(batch * n_ctx, d)
    for i in range(depth):
        use_na = i % n_mixers == 0
        j = i // n_mixers
        update_ctx = i < last_na
        if use_na:
            cast = (ffn_w_in, ffn_w_out, i) if w_in[i] is None else None
            q, k, vt, *converted = _qkv_proj(xl, ada, i, lat_row_big, w_qkv, j, tm_big, cast=cast)
            if converted:
                w_in[i], w_out[i] = converted
            q_c, k_c, vt_c = _qkv_proj(xc, ada, i, ctx_row, w_qkv, j, batch * n_ctx)
            att = _neighbourhood_attention(q, k, vt, k_c, vt_c, bias_rows, j, batch)
            xl = _proj_ln(att, w_o, j, xl, ada, i, lat_row, lmg, lmb, alpha, tm)
            if update_ctx:
                att_c = _context_attention(q_c, k_c, vt_c, batch)
                xc = _proj_ln(att_c, w_o, j, xc, ada, i, ctx_row, lmg, lmb, alpha, ctx_tm)
        else:
            xl = _pool_mix(xl, ada, i, lat_row, w_pool, ps, j, lmg, lmb, alpha, tm, seq)
            if update_ctx:
                xc = _pool_mix(xc, ada, i, ctx_row, w_pool, ps, j, lmg, lmb, alpha, n_ctx, n_ctx)
        if w_in[i] is None:
            w_in[i], w_out[i] = ffn_w_in[i].astype(BF16), ffn_w_out[i].astype(BF16)
        cast = (ffn_w_in, ffn_w_out, i + 1) if i + 1 < depth else None
        xl, *converted = _ffn(xl, ada, i, lat_row_big, w_in[i], w_out[i], lfg, lfb, alpha, tm_big, cast=cast)
        if converted:
            w_in[i + 1], w_out[i + 1] = converted
        if update_ctx:
            xc, = _ffn(xc, ada, i, ctx_row, w_in[i], w_out[i], lfg, lfb, alpha, min(tm_big, batch * n_ctx))
    return xl.reshape(batch, seq, d)
```

```python
import functools
import math

import numpy as np
import jax
import jax.numpy as jnp
from jax import lax
from jax.experimental import pallas as pl
from jax.experimental.pallas import tpu as pltpu

F32 = jnp.float32
BF16 = jnp.bfloat16

GRID_W = 64
N_HEADS = 16
HEAD_DIM = 128
WIN_ROWS = 8
WIN_COLS = 16
POOL_WINDOWS = (2, 4, 8, 16)
N_ADA = 6
LN_EPS = 1e-5
LOG2E = math.log2(math.e)

Q_ROWS = 4
K_ROWS = Q_ROWS + WIN_ROWS
Q_TOK = Q_ROWS * GRID_W
K_TOK = K_ROWS * GRID_W
N_BIAS_TILES = 18
NA_OUT_CHUNK = 2048
POOL_HALO = 8
COND_ROWS = 8
LANES = 128
ROW_CHUNK = 128
QKV_HEADS_PER_TILE = 4

VMEM_LIMIT = 60 * 1024 * 1024


def _cparams(*sem):
    return pltpu.CompilerParams(dimension_semantics=sem, vmem_limit_bytes=VMEM_LIMIT)


def _layer_norm(v, g, b):
    mu = jnp.mean(v, axis=-1, keepdims=True)
    d = v - mu
    var = jnp.mean(d * d, axis=-1, keepdims=True)
    return d * lax.rsqrt(var + LN_EPS) * g + b


def _silu(v):
    return v * jax.nn.sigmoid(v)


def _for_row_chunks(n_rows, fn):
    def body(c, carry):
        fn(pl.ds(pl.multiple_of(c * ROW_CHUNK, ROW_CHUNK), ROW_CHUNK))
        return carry
    lax.fori_loop(0, n_rows // ROW_CHUNK, body, 0)


def _modulate_rows(x_ref, sh_ref, sc_ref, h_ref):
    sc1 = 1.0 + sc_ref[...]
    sh = sh_ref[...]

    def chunk(rows):
        h_ref[rows, :] = (x_ref[rows, :] * sc1 + sh).astype(BF16)
    _for_row_chunks(x_ref.shape[0], chunk)


def _cast_side_job(w_in, w_out, layer, n_i, n_j):
    _, d, f2 = w_in.shape
    f = w_out.shape[1]
    bi, bj, ci = d // n_i, f2 // n_j, f // n_j
    in_specs = [pl.BlockSpec((None, bi, bj), lambda i, j: (layer, i, j)),
                pl.BlockSpec((None, ci, bi), lambda i, j: (layer, j, i))]
    out_specs = [pl.BlockSpec((bi, bj), lambda i, j: (i, j)), pl.BlockSpec((ci, bi), lambda i, j: (j, i))]
    out_shape = [jax.ShapeDtypeStruct((d, f2), BF16), jax.ShapeDtypeStruct((f, d), BF16)]
    return in_specs, out_specs, out_shape


def _run_cast_side_job(src_refs, dst_refs):
    for src, dst in zip(src_refs, dst_refs):
        dst[...] = src[...].astype(BF16)


def _ada_kernel(cond_ref, w_ref, b_ref, o_ref):
    s = _silu(cond_ref[...]).astype(BF16)
    o_ref[...] = jnp.dot(s, w_ref[...].astype(BF16), preferred_element_type=F32) + b_ref[...]


def _ada_params(cond, ada_w, ada_b, n_layers, tn=1024):
    depth, d, n = ada_w.shape
    return pl.pallas_call(
        _ada_kernel,
        grid=(n_layers, n // tn),
        in_specs=[
            pl.BlockSpec((COND_ROWS, d), lambda l, j: (0, 0)),
            pl.BlockSpec((None, d, tn), lambda l, j: (l, 0, j)),
            pl.BlockSpec((None, 1, tn), lambda l, j: (l, 0, j)),
        ],
        out_specs=pl.BlockSpec((None, COND_ROWS, tn), lambda l, j: (l, 0, j)),
        out_shape=jax.ShapeDtypeStruct((n_layers, COND_ROWS, n), F32),
        compiler_params=_cparams("parallel", "parallel"),
        name="ada_params",
    )(cond, ada_w, ada_b.reshape(depth, 1, n))


def _ada_side_job(cond, ada_w, ada_b, layer, n_steps, step_of):
    depth, d, n = ada_w.shape
    tn = LANES * pl.cdiv(n, n_steps * LANES)
    assert n % tn == 0
    blk = lambda *idx: jnp.minimum(step_of(*idx), n // tn - 1)
    in_specs = [pl.BlockSpec((COND_ROWS, d), lambda *idx: (0, 0)),
                pl.BlockSpec((None, d, tn), lambda *idx: (layer, 0, blk(*idx))),
                pl.BlockSpec((None, 1, tn), lambda *idx: (layer, 0, blk(*idx)))]
    out_spec = pl.BlockSpec((COND_ROWS, tn), lambda *idx: (0, blk(*idx)))
    out_shape = jax.ShapeDtypeStruct((COND_ROWS, n), F32)
    return in_specs, [out_spec], [out_shape], (cond, ada_w, ada_b.reshape(depth, 1, n))


def _ada_spec(which, row_fn, d):
    return pl.BlockSpec((None, None, 1, d), lambda i, *_: (row_fn(i), which, 0, 0))


def _vec_spec(layer, d):
    return pl.BlockSpec((None, 1, d), lambda i, *_: (layer, 0, 0))


def _qkv_kernel(x_ref, sh_ref, sc_ref, wq_ref, wk_ref, wv_ref, *rest, n_ada, n_cast):
    ada_src, rest = rest[:3 * n_ada], rest[3 * n_ada:]
    cast_src, rest = rest[:n_cast], rest[n_cast:]
    (q_ref, k_ref, vt_ref), rest = rest[:3], rest[3:]
    ada_dst, rest = rest[:n_ada], rest[n_ada:]
    cast_dst, (h_ref,) = rest[:n_cast], rest[n_cast:]

    @pl.when(pl.program_id(1) == 0)
    def _():
        _modulate_rows(x_ref, sh_ref, sc_ref, h_ref)

    if n_ada:
        _ada_kernel(*ada_src, *ada_dst)
    _run_cast_side_job(cast_src, cast_dst)
    h = h_ref[...]
    for w_ref, o_ref, transposed in ((wq_ref, q_ref, False), (wk_ref, k_ref, False), (wv_ref, vt_ref, True)):
        res = jnp.dot(h, w_ref[...], preferred_element_type=F32)
        for hh in range(o_ref.shape[0]):
            part = res[:, hh * HEAD_DIM:(hh + 1) * HEAD_DIM]
            o_ref[hh] = (part.T if transposed else part).astype(BF16)


def _qkv_proj(x, ada, row_fn, w_qkv, j_na, tm, ada_next=None, cast=None):
    m, d = x.shape
    hpt = QKV_HEADS_PER_TILE
    tn = hpt * HEAD_DIM
    n_groups = N_HEADS // hpt
    n_steps = (m // tm) * n_groups
    ada_in, ada_out, ada_shape, ada_args = (
        _ada_side_job(*ada_next, n_steps, lambda i, j: i * n_groups + j) if ada_next else ([], [], [], ()))
    cast_in, cast_out, cast_shape = _cast_side_job(*cast, m // tm, n_groups) if cast else ([], [], [])
    return pl.pallas_call(
        functools.partial(_qkv_kernel, n_ada=len(ada_out), n_cast=len(cast_out)),
        grid=(m // tm, n_groups),
        in_specs=[
            pl.BlockSpec((tm, d), lambda i, j: (i, 0)),
            _ada_spec(0, row_fn, d),
            _ada_spec(1, row_fn, d),
            pl.BlockSpec((None, d, tn), lambda i, j: (j_na, 0, j)),
            pl.BlockSpec((None, d, tn), lambda i, j: (j_na, 0, n_groups + j)),
            pl.BlockSpec((None, d, tn), lambda i, j: (j_na, 0, 2 * n_groups + j)),
        ] + ada_in + cast_in,
        out_specs=[
            pl.BlockSpec((hpt, tm, HEAD_DIM), lambda i, j: (j, i, 0)),
            pl.BlockSpec((hpt, tm, HEAD_DIM), lambda i, j: (j, i, 0)),
            pl.BlockSpec((hpt, HEAD_DIM, tm), lambda i, j: (j, 0, i)),
        ] + ada_out + cast_out,
        out_shape=[
            jax.ShapeDtypeStruct((N_HEADS, m, HEAD_DIM), BF16),
            jax.ShapeDtypeStruct((N_HEADS, m, HEAD_DIM), BF16),
            jax.ShapeDtypeStruct((N_HEADS, HEAD_DIM, m), BF16),
        ] + ada_shape + cast_shape,
        scratch_shapes=[pltpu.VMEM((tm, d), BF16)],
        compiler_params=_cparams("parallel", "arbitrary"),
        name="qkv_proj",
    )(x, ada, ada, w_qkv, w_qkv, w_qkv, *ada_args, *(cast[:2] if cast else ()))


def _tile_offset(e):
    return np.where(e == 16, -4, np.where(e == 17, 4, e - (WIN_ROWS - 1)))


def _bias_rows(rpb):
    n_dr, n_dc = rpb.shape[1], rpb.shape[2]
    rev = rpb[..., ::-1] * LOG2E
    d = _tile_offset(np.arange(N_BIAS_TILES))
    first = rev[:, np.clip(d + WIN_ROWS - 1, 0, n_dr - 1)]
    second = rev[:, np.clip(d + WIN_ROWS - 2, 0, n_dr - 1)]
    gap = jnp.zeros(first.shape[:2] + (GRID_W - n_dc,), rpb.dtype)
    return jnp.concatenate([first[..., WIN_COLS - 1:], gap, second, gap, first[..., :WIN_COLS - 1]], axis=-1)


def _bias_mask():
    e = np.arange(N_BIAS_TILES)[:, None, None]
    ck = np.arange(GRID_W)[None, :, None]
    lane = np.arange(2 * GRID_W)[None, None, :]
    half, cq = lane // GRID_W, lane % GRID_W
    dr = _tile_offset(e) - half
    row_ok = (np.abs(dr) <= WIN_ROWS - 1) & ((e != 16) | (half == 0)) & ((e != 17) | (half == 1))
    ws = np.clip(cq - WIN_COLS // 2, 0, GRID_W - WIN_COLS)
    return (row_ok & (ck >= ws) & (ck < ws + WIN_COLS)).astype(np.float32)


def _tile_entry(d):
    return d + WIN_ROWS - 1


_TILES_TOP = [[(j, _tile_entry(j - 2 * u)) for j in range(0, WIN_ROWS)] for u in range(Q_ROWS // 2)]
_TILES_BOT = [[(j, _tile_entry(j - WIN_ROWS - 2 * u)) for j in range(K_ROWS - WIN_ROWS, K_ROWS)]
              for u in range(Q_ROWS // 2)]
_TILES_MID = [[(j, 16 if j - 2 * u - 4 == -4 else 17 if j - 2 * u - 4 == 4 else _tile_entry(j - 2 * u - 4))
               for j in range(2 * u, 2 * u + WIN_ROWS + 1)] for u in range(Q_ROWS // 2)]


def _na_kernel(q_ref, k_ref, vt_ref, kc_ref, vct_ref, brow_ref, mask_ref, o_ref,
               tab_ref, st_ref, pt_ref, sct_ref, pct_ref, otw_ref, linv_ref, *, n_blocks, n_ctx):
    lanes = 2 * GRID_W
    n_ctx_tiles = n_ctx // GRID_W
    n_pairs = Q_ROWS // 2
    last = n_blocks - 1
    zero_tile = jnp.zeros((GRID_W, lanes), BF16)
    nt = (((1,), (1,)), ((), ()))

    def tok(b):
        return b * Q_TOK

    def key_block(b):
        return min(max(b - 1, 0), n_blocks - K_ROWS // Q_ROWS)

    def zero_unused(pslot, tiles):
        for u in range(n_pairs):
            used = dict(tiles[u])
            for j in range(K_ROWS):
                if j not in used:
                    pt_ref[pslot, j * GRID_W:(j + 1) * GRID_W, u * lanes:(u + 1) * lanes] = zero_tile

    def scores(b, sslot):
        q = q_ref[pl.ds(tok(b), Q_TOK), :]
        st_ref[sslot] = lax.dot_general(k_ref[pl.ds(tok(key_block(b)), K_TOK), :], q, nt,
                                        preferred_element_type=F32)

    def softmax(b, sslot, pslot, tiles):
        for u in range(n_pairs):
            cols = slice(u * lanes, (u + 1) * lanes)
            qcols = pl.ds(tok(b) + u * lanes, lanes)

            def tile_scores(i):
                if i < len(tiles[u]):
                    j, e = tiles[u][i]
                    return st_ref[sslot, j * GRID_W:(j + 1) * GRID_W, cols] + tab_ref[e]
                t = i - len(tiles[u])
                return sct_ref[t * GRID_W:(t + 1) * GRID_W, qcols]

            n_tiles = len(tiles[u]) + n_ctx_tiles
            m = tile_scores(0)
            for i in range(1, n_tiles):
                m = jnp.maximum(m, tile_scores(i))
            m = jnp.max(m, axis=0, keepdims=True)
            l = None
            for i in range(n_tiles):
                p = jnp.exp2(tile_scores(i) - m)
                l = p if l is None else l + p
                if i < len(tiles[u]):
                    j = tiles[u][i][0]
                    pt_ref[pslot, j * GRID_W:(j + 1) * GRID_W, cols] = p.astype(BF16)
                else:
                    t = i - len(tiles[u])
                    pct_ref[t * GRID_W:(t + 1) * GRID_W, qcols] = p.astype(BF16)
            linv_ref[0:1, qcols] = 1.0 / jnp.sum(l, axis=0, keepdims=True)

    def values(b, pslot):
        otw_ref[:, pl.ds(tok(b), Q_TOK)] = jnp.dot(vt_ref[:, pl.ds(tok(key_block(b)), K_TOK)], pt_ref[pslot],
                                                   preferred_element_type=F32)

    @pl.when(pl.program_id(1) == 0)
    def _():
        for e in range(N_BIAS_TILES):
            row = jnp.broadcast_to(brow_ref[e:e + 1, :], (GRID_W, lanes))
            toeplitz = pltpu.roll(row, 0, axis=1, stride=1, stride_axis=0)
            tab_ref[e] = jnp.where(mask_ref[e] != 0, toeplitz, -jnp.inf)

    zero_unused(0, _TILES_MID)
    zero_unused(1, _TILES_MID)
    zero_unused(2, _TILES_TOP)
    zero_unused(3, _TILES_BOT)
    sct_ref[...] = lax.dot_general(kc_ref[...], q_ref[...], nt, preferred_element_type=F32)

    scores(0, 0)
    scores(1, 1)
    softmax(0, 0, 2, _TILES_TOP)
    scores(2, 0)
    softmax(1, 1, 1, _TILES_MID)
    values(0, 2)

    finished = [0]

    def finish(n_done):
        while (finished[0] + 1) * NA_OUT_CHUNK <= n_done * Q_TOK:
            cols = slice(finished[0] * NA_OUT_CHUNK, (finished[0] + 1) * NA_OUT_CHUNK)
            ot = otw_ref[:, cols] + jnp.dot(vct_ref[...], pct_ref[:, cols], preferred_element_type=F32)
            o_ref[cols, :] = (ot * linv_ref[0:1, cols]).T.astype(BF16)
            finished[0] += 1

    for b in range(2, last - 1, 2):
        scores(b + 1, 1)
        softmax(b, 0, 0, _TILES_MID)
        values(b - 1, 1)
        finish(b)
        scores(b + 2, 0)
        softmax(b + 1, 1, 1, _TILES_MID)
        values(b, 0)
        finish(b + 1)
    scores(last, 1)
    softmax(last - 1, 0, 0, _TILES_MID)
    values(last - 2, 1)
    finish(last - 1)
    softmax(last, 1, 3, _TILES_BOT)
    values(last - 1, 0)
    values(last, 3)
    finish(n_blocks)


def _neighbourhood_attention(q, k, vt, k_c, vt_c, bias_rows, j_na, batch):
    m = q.shape[1]
    seq = m // batch
    n_ctx = k_c.shape[1] // batch
    n_blocks = seq // Q_TOK
    assert n_blocks % 2 == 0 and n_blocks >= 4 and seq % NA_OUT_CHUNK == 0
    kern = functools.partial(_na_kernel, n_blocks=n_blocks, n_ctx=n_ctx)
    return pl.pallas_call(
        kern,
        grid=(N_HEADS, batch),
        in_specs=[
            pl.BlockSpec((None, seq, HEAD_DIM), lambda h, b: (h, b, 0)),
            pl.BlockSpec((None, seq, HEAD_DIM), lambda h, b: (h, b, 0)),
            pl.BlockSpec((None, HEAD_DIM, seq), lambda h, b: (h, 0, b)),
            pl.BlockSpec((None, n_ctx, HEAD_DIM), lambda h, b: (h, b, 0)),
            pl.BlockSpec((None, HEAD_DIM, n_ctx), lambda h, b: (h, 0, b)),
            pl.BlockSpec((None, N_BIAS_TILES, 2 * GRID_W), lambda h, b: (j_na * N_HEADS + h, 0, 0)),
            pl.BlockSpec((N_BIAS_TILES, GRID_W, 2 * GRID_W), lambda h, b: (0, 0, 0)),
        ],
        out_specs=pl.BlockSpec((None, seq, HEAD_DIM), lambda h, b: (h, b, 0)),
        out_shape=jax.ShapeDtypeStruct((N_HEADS, m, HEAD_DIM), BF16),
        scratch_shapes=[
            pltpu.VMEM((N_BIAS_TILES, GRID_W, 2 * GRID_W), F32),
            pltpu.VMEM((2, K_TOK, Q_TOK), F32),
            pltpu.VMEM((4, K_TOK, Q_TOK), BF16),
            pltpu.VMEM((n_ctx, seq), F32),
            pltpu.VMEM((n_ctx, seq), BF16),
            pltpu.VMEM((HEAD_DIM, seq), F32),
            pltpu.VMEM((8, seq), F32),
        ],
        compiler_params=_cparams("parallel", "arbitrary"),
        name="neighbourhood_attention",
    )(q, k, vt, k_c, vt_c, bias_rows, jnp.asarray(_bias_mask()))


def _ctx_attn_kernel(q_ref, k_ref, vt_ref, o_ref):
    nt = (((1,), (1,)), ((), ()))
    for h in range(q_ref.shape[0]):
        st = lax.dot_general(k_ref[h], q_ref[h], nt, preferred_element_type=F32)
        p = jnp.exp2(st - jnp.max(st, axis=0, keepdims=True))
        inv_l = 1.0 / jnp.sum(p, axis=0, keepdims=True)
        ot = jnp.dot(vt_ref[h], p.astype(BF16), preferred_element_type=F32) * inv_l
        o_ref[h] = ot.T.astype(BF16)


def _context_attention(q_c, k_c, vt_c, batch):
    m = q_c.shape[1]
    n_ctx = m // batch
    return pl.pallas_call(
        _ctx_attn_kernel,
        grid=(batch,),
        in_specs=[
            pl.BlockSpec((N_HEADS, n_ctx, HEAD_DIM), lambda b: (0, b, 0)),
            pl.BlockSpec((N_HEADS, n_ctx, HEAD_DIM), lambda b: (0, b, 0)),
            pl.BlockSpec((N_HEADS, HEAD_DIM, n_ctx), lambda b: (0, 0, b)),
        ],
        out_specs=pl.BlockSpec((N_HEADS, n_ctx, HEAD_DIM), lambda b: (0, b, 0)),
        out_shape=jax.ShapeDtypeStruct((N_HEADS, m, HEAD_DIM), BF16),
        compiler_params=_cparams("parallel"),
        name="context_attention",
    )(q_c, k_c, vt_c)


def _proj_ln_kernel(a_ref, w_ref, x_ref, g_ref, lng_ref, lnb_ref, o_ref, acat_ref, *, alpha):
    for h in range(N_HEADS):
        acat_ref[:, h * HEAD_DIM:(h + 1) * HEAD_DIM] = a_ref[h]
    y = jnp.dot(acat_ref[...], w_ref[...], preferred_element_type=F32)
    o_ref[...] = _layer_norm(alpha * x_ref[...] + g_ref[...] * y, lng_ref[...], lnb_ref[...])


def _proj_ln(a, w_o, j_na, x, ada, layer, row_fn, ln_g, ln_b, alpha, tm):
    m, d = x.shape
    return pl.pallas_call(
        functools.partial(_proj_ln_kernel, alpha=alpha),
        grid=(m // tm,),
        in_specs=[
            pl.BlockSpec((N_HEADS, tm, HEAD_DIM), lambda i: (0, i, 0)),
            pl.BlockSpec((None, d, d), lambda i: (j_na, 0, 0)),
            pl.BlockSpec((tm, d), lambda i: (i, 0)),
            _ada_spec(2, row_fn, d),
            _vec_spec(layer, d),
            _vec_spec(layer, d),
        ],
        out_specs=pl.BlockSpec((tm, d), lambda i: (i, 0)),
        out_shape=jax.ShapeDtypeStruct((m, d), F32),
        scratch_shapes=[pltpu.VMEM((tm, d), BF16)],
        compiler_params=_cparams("parallel"),
        name="out_proj_norm",
    )(a, w_o, x, ada, ln_g, ln_b)


def _pool_kernel(x_ref, xp_ref, xn_ref, sh_ref, sc_ref, g_ref, w_ref, ps_ref, lng_ref, lnb_ref, *rest,
                 alpha, seq_len, n_ada):
    ada_src, o_ref, ada_dst, ext_ref = rest[:3 * n_ada], rest[3 * n_ada], rest[3 * n_ada + 1:-1], rest[-1]
    if n_ada:
        _ada_kernel(*ada_src, *ada_dst)
    tm, d = x_ref.shape
    gw = d // len(POOL_WINDOWS)
    start = pl.program_id(0) * tm
    pos0 = start % seq_len
    sc1 = 1.0 + sc_ref[...]
    sh = sh_ref[...]
    h = x_ref[...] * sc1 + sh
    ext_ref[0:POOL_HALO, :] = jnp.where(pos0 > 0, xp_ref[...] * sc1 + sh, 0.0)
    ext_ref[POOL_HALO:POOL_HALO + tm, :] = h
    ext_ref[POOL_HALO + tm:, :] = jnp.where(pos0 + tm < seq_len, xn_ref[...] * sc1 + sh, 0.0)
    n_ext = tm + 2 * POOL_HALO
    pos = pos0 + lax.broadcasted_iota(jnp.int32, (tm, LANES), 0)
    ys = []
    for gi, w in enumerate(POOL_WINDOWS):
        cols = slice(gi * gw, (gi + 1) * gw)
        ext = ext_ref[:, cols]
        s = ext + pltpu.roll(ext, 1, axis=0)
        half = 1
        while 2 * half < w:
            s = pltpu.roll(s, half, axis=0) + pltpu.roll(s, n_ext - half, axis=0)
            half *= 2
        tot = s[POOL_HALO:POOL_HALO + tm]
        cnt = jnp.minimum(pos - w // 2 + w, seq_len) - jnp.maximum(pos - w // 2, 0)
        inv_cnt = jnp.tile(1.0 / cnt.astype(F32), (1, gw // LANES))
        pooled = tot * inv_cnt - h[:, cols]
        ys.append(jnp.dot(pooled.astype(BF16), w_ref[gi], preferred_element_type=F32))
    y = jnp.concatenate(ys, axis=1) * ps_ref[...]
    o_ref[...] = _layer_norm(alpha * x_ref[...] + g_ref[...] * y, lng_ref[...], lnb_ref[...])


def _pool_mix(x, ada, layer, row_fn, w_pool, pool_scale, j_pool, ln_g, ln_b, alpha, tm, seq_len, ada_next=None):
    m, d = x.shape
    hb = tm // POOL_HALO
    n_halo_blocks = m // POOL_HALO
    gw = d // len(POOL_WINDOWS)
    ada_in, ada_out, ada_shape, ada_args = (
        _ada_side_job(*ada_next, m // tm, lambda i: i) if ada_next else ([], [], [], ()))
    return pl.pallas_call(
        functools.partial(_pool_kernel, alpha=alpha, seq_len=seq_len, n_ada=len(ada_out)),
        grid=(m // tm,),
        in_specs=[
            pl.BlockSpec((tm, d), lambda i: (i, 0)),
            pl.BlockSpec((POOL_HALO, d), lambda i: (jnp.maximum(i * hb - 1, 0), 0)),
            pl.BlockSpec((POOL_HALO, d), lambda i: (jnp.minimum((i + 1) * hb, n_halo_blocks - 1), 0)),
            _ada_spec(0, row_fn, d),
            _ada_spec(1, row_fn, d),
            _ada_spec(2, row_fn, d),
            pl.BlockSpec((None, len(POOL_WINDOWS), gw, gw), lambda i: (j_pool, 0, 0, 0)),
            _vec_spec(j_pool, d),
            _vec_spec(layer, d),
            _vec_spec(layer, d),
        ] + ada_in,
        out_specs=[pl.BlockSpec((tm, d), lambda i: (i, 0))] + ada_out,
        out_shape=[jax.ShapeDtypeStruct((m, d), F32)] + ada_shape,
        scratch_shapes=[pltpu.VMEM((tm + 2 * POOL_HALO, d), F32)],
        compiler_params=_cparams("parallel"),
        name="pool_mix_norm",
    )(x, x, x, ada, ada, ada, w_pool, pool_scale, ln_g, ln_b, *ada_args)


def _ffn_kernel(x_ref, sh_ref, sc_ref, g_ref, wg_ref, wu_ref, wo_ref, lng_ref, lnb_ref, *rest, alpha):
    n_cast = (len(rest) - 2) // 2
    cast_src, o_ref, cast_dst, h_ref = rest[:n_cast], rest[n_cast], rest[n_cast + 1:-1], rest[-1]
    f = pl.program_id(1)

    def partial_product():
        _run_cast_side_job(cast_src, cast_dst)
        h = h_ref[...]
        gate = jnp.dot(h, wg_ref[...], preferred_element_type=F32)
        up = jnp.dot(h, wu_ref[...], preferred_element_type=F32)
        return jnp.dot((_silu(gate) * up).astype(BF16), wo_ref[...], preferred_element_type=F32)

    @pl.when(f == 0)
    def _():
        _modulate_rows(x_ref, sh_ref, sc_ref, h_ref)
        o_ref[...] = partial_product()

    @pl.when(f > 0)
    def _():
        o_ref[...] += partial_product()

    @pl.when(f == pl.num_programs(1) - 1)
    def _():
        g, lng, lnb = g_ref[...], lng_ref[...], lnb_ref[...]

        def chunk(rows):
            o_ref[rows, :] = _layer_norm(alpha * x_ref[rows, :] + g * o_ref[rows, :], lng, lnb)
        _for_row_chunks(o_ref.shape[0], chunk)


def _ffn(x, ada, layer, row_fn, w_in, w_out, ln_g, ln_b, alpha, tm, tf=512, cast=None):
    m, d = x.shape
    d_ff = w_out.shape[0]
    nf = d_ff // tf
    cast_in, cast_out, cast_shape = _cast_side_job(*cast, m // tm, nf) if cast else ([], [], [])
    return pl.pallas_call(
        functools.partial(_ffn_kernel, alpha=alpha),
        grid=(m // tm, nf),
        in_specs=[
            pl.BlockSpec((tm, d), lambda i, f: (i, 0)),
            _ada_spec(3, row_fn, d),
            _ada_spec(4, row_fn, d),
            _ada_spec(5, row_fn, d),
            pl.BlockSpec((d, tf), lambda i, f: (0, f)),
            pl.BlockSpec((d, tf), lambda i, f: (0, nf + f)),
            pl.BlockSpec((tf, d), lambda i, f: (f, 0)),
            _vec_spec(layer, d),
            _vec_spec(layer, d),
        ] + cast_in,
        out_specs=[pl.BlockSpec((tm, d), lambda i, f: (i, 0))] + cast_out,
        out_shape=[jax.ShapeDtypeStruct((m, d), F32)] + cast_shape,
        scratch_shapes=[pltpu.VMEM((tm, d), BF16)],
        compiler_params=_cparams("parallel", "arbitrary"),
        name="ffn_norm",
    )(x, ada, ada, ada, w_in, w_in, w_out, ln_g, ln_b, *(cast[:2] if cast else ()))


def kernel(x, c, ctx, c_ctx, ada_w, ada_b, ln_mix_g, ln_mix_b, ln_ffn_g, ln_ffn_b,
           na_w_qkv, na_w_o, na_rpb, pool_w, pool_scale, ffn_w_in, ffn_w_out):
    batch, seq, d = x.shape
    n_ctx = ctx.shape[1]
    depth = ada_w.shape[0]
    n_mixers = 2
    assert seq % Q_TOK == 0 and seq == GRID_W * GRID_W and batch < COND_ROWS
    alpha = (2 * depth) ** 0.25
    last_na = max(i for i in range(depth) if i % n_mixers == 0)

    cond = jnp.concatenate([c, c_ctx[None], jnp.zeros((COND_ROWS - batch - 1, d), F32)], axis=0)
    as_table = lambda t: t.reshape(COND_ROWS, N_ADA, 1, d)
    ada_next_table = _ada_params(cond, ada_w, ada_b, 1)

    tm = 512
    tm_big = 1024
    lat_row = lambda i: i // (seq // tm)
    lat_row_big = lambda i: i // (seq // tm_big)
    ctx_row = lambda i: batch
    ctx_tm = min(tm, batch * n_ctx)

    qscale = jnp.concatenate([jnp.full((d,), HEAD_DIM ** -0.5 * LOG2E, F32), jnp.ones((2 * d,), F32)])
    w_qkv = (na_w_qkv * qscale).astype(BF16)
    w_o = na_w_o.astype(BF16)
    w_pool = pool_w.astype(BF16)
    w_in = [None] * depth
    w_out = [None] * depth
    lmg, lmb = ln_mix_g.reshape(depth, 1, d), ln_mix_b.reshape(depth, 1, d)
    lfg, lfb = ln_ffn_g.reshape(depth, 1, d), ln_ffn_b.reshape(depth, 1, d)
    ps = pool_scale.reshape(-1, 1, d)
    bias_rows = _bias_rows(na_rpb.reshape((-1,) + na_rpb.shape[2:]))

    xl = x.reshape(batch * seq, d)
    xc = ctx.reshape(batch * n_ctx, d)
    for i in range(depth):
        use_na = i % n_mixers == 0
        j = i // n_mixers
        update_ctx = i < last_na
        ada = as_table(ada_next_table)
        ada_next = (cond, ada_w, ada_b, i + 1) if i + 1 < depth else None
        if use_na:
            cast = (ffn_w_in, ffn_w_out, i) if w_in[i] is None else None
            q, k, vt, *extra = _qkv_proj(xl, ada, lat_row_big, w_qkv, j, tm_big, ada_next=ada_next, cast=cast)
            if ada_next:
                ada_next_table = extra.pop(0)
            if cast:
                w_in[i], w_out[i] = extra
            q_c, k_c, vt_c = _qkv_proj(xc, ada, ctx_row, w_qkv, j, batch * n_ctx)
            att = _neighbourhood_attention(q, k, vt, k_c, vt_c, bias_rows, j, batch)
            xl = _proj_ln(att, w_o, j, xl, ada, i, lat_row, lmg, lmb, alpha, tm)
            if update_ctx:
                att_c = _context_attention(q_c, k_c, vt_c, batch)
                xc = _proj_ln(att_c, w_o, j, xc, ada, i, ctx_row, lmg, lmb, alpha, ctx_tm)
        else:
            xl, *extra = _pool_mix(xl, ada, i, lat_row, w_pool, ps, j, lmg, lmb, alpha, tm, seq, ada_next=ada_next)
            if ada_next:
                ada_next_table, = extra
            if update_ctx:
                xc, = _pool_mix(xc, ada, i, ctx_row, w_pool, ps, j, lmg, lmb, alpha, n_ctx, n_ctx)
        if w_in[i] is None:
            w_in[i], w_out[i] = ffn_w_in[i].astype(BF16), ffn_w_out[i].astype(BF16)
        cast = (ffn_w_in, ffn_w_out, i + 1) if i + 1 < depth else None
        xl, *converted = _ffn(xl, ada, i, lat_row_big, w_in[i], w_out[i], lfg, lfb, alpha, tm_big, cast=cast)
        if converted:
            w_in[i + 1], w_out[i + 1] = converted
        if update_ctx:
            xc, = _ffn(xc, ada, i, ctx_row, w_in[i], w_out[i], lfg, lfb, alpha, min(tm_big, batch * n_ctx))
    return xl.reshape(batch, seq, d)
```
